```python
import jax, jax.numpy as jnp
from jax import lax
import numpy as np

D_MODEL = 2048
BATCH = 1
SEQ = 8192
DEPTH = 4
DEC_BATCH = 8
DEC_SEQ = 2048
PAST_LEN = 128

N_MIXERS = 2
N_ATTN_LAYERS = (DEPTH + 1) // 2
N_CONV_LAYERS = DEPTH // 2
HEAD_DIM = 128
N_Q_HEADS = D_MODEL // HEAD_DIM
N_KV_HEADS = N_Q_HEADS // 4
GROUP = N_Q_HEADS // N_KV_HEADS
ATTN_WIDTH = N_Q_HEADS * HEAD_DIM
KV_WIDTH = N_KV_HEADS * HEAD_DIM
ATTN_IN = 2 * ATTN_WIDTH + 2 * KV_WIDTH
WINDOW = 128
BLOCK = 128
CONV_WIDTH = D_MODEL
CONV_K = 3
N_META = 16
LEAD = BLOCK
ROPE_THETA = 10000.0
EPS = 1e-6
NEG = -1e30

kernel_name = "hybrid_swa_shortconv_encoder"


def rmsnorm(x, g):
    xf = x.astype(jnp.float32)
    y = xf * lax.rsqrt(jnp.mean(xf * xf, axis=-1, keepdims=True) + EPS)
    return (y * g.astype(jnp.float32)).astype(x.dtype)


def rope(x, pos):
    half = HEAD_DIM // 2
    inv_freq = ROPE_THETA ** (-jnp.arange(0, half, dtype=jnp.float32) * (2.0 / HEAD_DIM))
    ang = pos.astype(jnp.float32)[:, None] * inv_freq[None, :]
    cos = jnp.concatenate([jnp.cos(ang), jnp.cos(ang)], -1)[None, :, None, :]
    sin = jnp.concatenate([jnp.sin(ang), jnp.sin(ang)], -1)[None, :, None, :]
    xf = x.astype(jnp.float32)
    rot = jnp.concatenate([-xf[..., half:], xf[..., :half]], -1)
    return (xf * cos + rot * sin).astype(x.dtype)


def band_blocks(t):
    b, l = t.shape[0], t.shape[1]
    nb = l // BLOCK
    tb = t.reshape((b, nb, BLOCK) + t.shape[2:])
    tp = jnp.pad(tb, [(0, 0), (1, 1)] + [(0, 0)] * (tb.ndim - 2))
    return jnp.concatenate([tp[:, :-2], tp[:, 1:-1], tp[:, 2:]], axis=2)


def attn_branch(h, pos, tok_ok, w_in, w_out, sink):
    B, L, _ = h.shape
    nb = L // BLOCK
    proj = h @ w_in
    q, k, v, z = jnp.split(proj, [ATTN_WIDTH, ATTN_WIDTH + KV_WIDTH, ATTN_WIDTH + 2 * KV_WIDTH], axis=-1)
    q = rope(q.reshape(B, L, N_Q_HEADS, HEAD_DIM), pos)
    k = rope(k.reshape(B, L, N_KV_HEADS, HEAD_DIM), pos)
    v = v.reshape(B, L, N_KV_HEADS, HEAD_DIM)
    qb = q.reshape(B, nb, BLOCK, N_KV_HEADS, GROUP, HEAD_DIM)
    kb, vb = band_blocks(k), band_blocks(v)
    k_meta = k[:, LEAD - N_META:LEAD]
    v_meta = v[:, LEAD - N_META:LEAD]
    pq = pos.reshape(nb, BLOCK)
    pk = band_blocks(pos[None])[0]
    okk = band_blocks(tok_ok[None])[0]
    band_mask = okk[:, None, :] & (jnp.abs(pq[:, :, None] - pk[:, None, :]) <= WINDOW)
    p_meta = pos[LEAD - N_META:LEAD]
    meta_mask = jnp.abs(pq[:, :, None] - p_meta[None, None, :]) > WINDOW
    mask = jnp.concatenate([band_mask, meta_mask], axis=-1)[None, :, None, None]
    scale = HEAD_DIM ** -0.5
    s_band = jnp.einsum('bnqhgd,bnkhd->bnhgqk', qb, kb)
    s_meta = jnp.einsum('bnqhgd,bkhd->bnhgqk', qb, k_meta)
    s = jnp.concatenate([s_band, s_meta], axis=-1).astype(jnp.float32) * scale
    s = jnp.where(mask, s, NEG)
    sink_l = sink.astype(jnp.float32).reshape(N_KV_HEADS, GROUP)[None, None, :, :, None, None]
    m = jnp.maximum(jnp.max(s, axis=-1, keepdims=True), sink_l)
    e = jnp.exp(s - m)
    p = (e / (jnp.sum(e, axis=-1, keepdims=True) + jnp.exp(sink_l - m))).astype(v.dtype)
    o = (jnp.einsum('bnhgqk,bnkhd->bnqhgd', p[..., :3 * BLOCK], vb)
         + jnp.einsum('bnhgqk,bkhd->bnqhgd', p[..., 3 * BLOCK:], v_meta))
    o = o.reshape(B, L, ATTN_WIDTH)
    return (o * jax.nn.silu(z)) @ w_out


def conv_branch(h, tok_ok, w_in, conv_w, w_out):
    proj = h @ w_in
    b_gate, c_gate, u, z = jnp.split(proj, 4, axis=-1)
    u = jnp.where(tok_ok[None, :, None], c_gate * u, 0.0)
    up = jnp.pad(u, ((0, 0), (1, 1), (0, 0)))
    y = up[:, :-2] * conv_w[0] + up[:, 1:-1] * conv_w[1] + up[:, 2:] * conv_w[2]
    return ((b_gate * y) * jax.nn.silu(z)) @ w_out


def trunk(x, meta_tokens, norm_w, attn_w_in, attn_w_out, attn_sink, conv_w_in, conv_w, conv_w_out, final_norm_w):
    B, S, D = x.shape
    lead = jnp.concatenate([jnp.zeros((LEAD - N_META, D), x.dtype), meta_tokens.astype(x.dtype)], axis=0)
    h = jnp.concatenate([jnp.broadcast_to(lead[None], (B, LEAD, D)), x], axis=1)
    L = LEAD + S
    pos = jnp.arange(L, dtype=jnp.int32) - (LEAD - N_META)
    tok_ok = pos >= 0
    for i in range(DEPTH):
        hn = rmsnorm(h, norm_w[i])
        j = i // N_MIXERS
        if i % N_MIXERS == 0:
            h = h + attn_branch(hn, pos, tok_ok, attn_w_in[j], attn_w_out[j], attn_sink[j])
        else:
            h = h + conv_branch(hn, tok_ok, conv_w_in[j], conv_w[j], conv_w_out[j])
    return rmsnorm(h, final_norm_w)[:, LEAD:]


def setup_inputs(seed: int = 0) -> dict:
    key = jax.random.key(seed)
    ks = jax.random.split(key, 12)
    f32 = jnp.float32
    return {
        "x_prompt": jax.random.normal(ks[0], (BATCH, SEQ, D_MODEL), f32),
        "x_sample": jax.random.normal(ks[1], (DEC_BATCH, DEC_SEQ, D_MODEL), f32),
        "meta_tokens": jax.random.normal(ks[2], (N_META, D_MODEL), f32),
        "norm_w": 1.0 + 0.02 * jax.random.normal(ks[3], (DEPTH, D_MODEL), f32),
        "attn_w_in": jax.random.normal(ks[4], (N_ATTN_LAYERS, D_MODEL, ATTN_IN), f32) * D_MODEL ** -0.5,
        "attn_w_out": jax.random.normal(ks[5], (N_ATTN_LAYERS, ATTN_WIDTH, D_MODEL), f32) * ATTN_WIDTH ** -0.5,
        "attn_sink": 0.5 * jax.random.normal(ks[6], (N_ATTN_LAYERS, N_Q_HEADS), f32),
        "conv_w_in": jax.random.normal(ks[7], (N_CONV_LAYERS, D_MODEL, 4 * CONV_WIDTH), f32) * D_MODEL ** -0.5,
        "conv_w": jax.random.normal(ks[8], (N_CONV_LAYERS, CONV_K, CONV_WIDTH), f32) * CONV_K ** -0.5,
        "conv_w_out": jax.random.normal(ks[9], (N_CONV_LAYERS, CONV_WIDTH, D_MODEL), f32) * CONV_WIDTH ** -0.5,
        "final_norm_w": 1.0 + 0.02 * jax.random.normal(ks[10], (D_MODEL,), f32),
    }


def reference(x_prompt, x_sample, meta_tokens, norm_w, attn_w_in, attn_w_out, attn_sink, conv_w_in, conv_w, conv_w_out, final_norm_w):
    y_prompt = trunk(x_prompt, meta_tokens, norm_w, attn_w_in, attn_w_out, attn_sink, conv_w_in, conv_w, conv_w_out, final_norm_w)
    y_sample = trunk(x_sample, meta_tokens, norm_w, attn_w_in, attn_w_out, attn_sink, conv_w_in, conv_w, conv_w_out, final_norm_w)
    return (y_prompt, y_sample)
```

```python
import functools

import jax
import jax.numpy as jnp
from jax import lax
from jax.experimental import pallas as pl
from jax.experimental.pallas import tpu as pltpu

D_MODEL = 2048
HEAD_DIM = 128
N_Q_HEADS = 16
N_KV_HEADS = 4
GROUP = N_Q_HEADS // N_KV_HEADS
ATTN_WIDTH = N_Q_HEADS * HEAD_DIM
KV_WIDTH = N_KV_HEADS * HEAD_DIM
ATTN_IN = 2 * ATTN_WIDTH + 2 * KV_WIDTH
CONV_WIDTH = D_MODEL
WINDOW = 128
BLOCK = 128
N_META = 16
LEAD = BLOCK
ROPE_THETA = 10000.0
EPS = 1e-6
NEG = -1e30

ROW_TILE = 384
COL_CHUNK = 512
BF16_SUBLANES = 16
VMEM_LIMIT = 58 * 1024 * 1024

F32 = jnp.float32
BF16 = jnp.bfloat16


def _params(n_axes):
    return pltpu.CompilerParams(
        dimension_semantics=("arbitrary",) * n_axes, vmem_limit_bytes=VMEM_LIMIT)


def _resident(shape):
    return pl.BlockSpec(shape, lambda *_: (0,) * len(shape), pipeline_mode=pl.Buffered(1))


def _rmsnorm_bf16(x, g):
    ms = jnp.mean(x * x, axis=-1, keepdims=True)
    return ((x * lax.rsqrt(ms + EPS)) * g).astype(BF16)


def _silu(z):
    return z * jax.nn.sigmoid(z)


def _attn_inproj_kernel(x_ref, g_ref, w_ref, cos_ref, sin_ref, o_ref, hn_ref):
    hn_ref[...] = _rmsnorm_bf16(x_ref[...], g_ref[...])
    cos = cos_ref[...]
    sin = sin_ref[...]
    n_rope = (ATTN_WIDTH + KV_WIDTH) // COL_CHUNK
    for c in range(ATTN_IN // COL_CHUNK):
        cols = slice(c * COL_CHUNK, (c + 1) * COL_CHUNK)
        acc = jnp.dot(hn_ref[...], w_ref[:, cols], preferred_element_type=F32)
        if c < n_rope:
            heads = []
            for hh in range(COL_CHUNK // HEAD_DIM):
                xh = acc[:, hh * HEAD_DIM:(hh + 1) * HEAD_DIM]
                heads.append(xh * cos + pltpu.roll(xh, HEAD_DIM // 2, axis=1) * sin)
            acc = jnp.concatenate(heads, axis=1)
        o_ref[:, cols] = acc.astype(BF16)


def _attn_inproj(h, g, w, cos, sin):
    rows = h.shape[0]
    return pl.pallas_call(
        _attn_inproj_kernel,
        out_shape=jax.ShapeDtypeStruct((rows, ATTN_IN), BF16),
        grid=(rows // ROW_TILE,),
        in_specs=[
            pl.BlockSpec((ROW_TILE, D_MODEL), lambda i: (i, 0)),
            _resident((1, D_MODEL)),
            _resident((D_MODEL, ATTN_IN)),
            pl.BlockSpec((ROW_TILE, HEAD_DIM), lambda i: (i, 0)),
            pl.BlockSpec((ROW_TILE, HEAD_DIM), lambda i: (i, 0)),
        ],
        out_specs=pl.BlockSpec((ROW_TILE, ATTN_IN), lambda i: (i, 0)),
        scratch_shapes=[pltpu.VMEM((ROW_TILE, D_MODEL), BF16)],
        compiler_params=_params(1),
        name="attn_inproj",
    )(h, g, w, cos, sin)


def _attn_core_kernel(prev_ref, next_ref, lead_ref, jblk_ref, nblk_ref, sink_ref,
                      q_ref, kp_ref, ko_ref, kn_ref, km_ref,
                      vp_ref, vo_ref, vn_ref, vm_ref, z_ref, o_ref):
    b = pl.program_id(0)
    h = pl.program_id(1)
    j = jblk_ref[b]
    n = nblk_ref[b]
    kcat = jnp.concatenate([kp_ref[...], ko_ref[...], kn_ref[...], km_ref[...]], axis=0)
    vcat = jnp.concatenate([vp_ref[...], vo_ref[...], vn_ref[...], vm_ref[...]], axis=0)
    nk = 3 * BLOCK + N_META

    r = lax.broadcasted_iota(jnp.int32, (BLOCK, nk), 0)
    c = lax.broadcasted_iota(jnp.int32, (BLOCK, nk), 1)
    part = c // BLOCK
    cc = c % BLOCK
    pos_q = j * BLOCK + r - (LEAD - N_META)
    pos_k = (j - 1 + part) * BLOCK + cc - (LEAD - N_META)
    kblk = j - 1 + part
    band = ((kblk >= 0) & (kblk < n) & (pos_k >= 0)
            & (jnp.abs(pos_q - pos_k) <= WINDOW))
    meta = jnp.abs(pos_q - cc) > WINDOW
    mask = ((part == 3) & meta) | ((part < 3) & band)

    scale = HEAD_DIM ** -0.5
    for g in range(GROUP):
        cols = slice(g * HEAD_DIM, (g + 1) * HEAD_DIM)
        s = lax.dot_general(q_ref[:, cols], kcat, (((1,), (1,)), ((), ())),
                            preferred_element_type=F32) * scale
        s = jnp.where(mask, s, NEG)
        sink = sink_ref[h * GROUP + g]
        m = jnp.maximum(jnp.max(s, axis=-1, keepdims=True), sink)
        e = jnp.exp(s - m)
        denom = jnp.sum(e, axis=-1, keepdims=True) + jnp.exp(sink - m)
        p = (e / denom).astype(BF16)
        o = jnp.dot(p, vcat, preferred_element_type=F32)
        z = z_ref[:, cols].astype(F32)
        o_ref[:, cols] = (o * _silu(z)).astype(BF16)


def _attn_core(proj, sink, tables):
    rows = proj.shape[0]
    nblocks = rows // BLOCK
    gw = GROUP * HEAD_DIM
    k_col = ATTN_WIDTH // HEAD_DIM
    v_col = (ATTN_WIDTH + KV_WIDTH) // HEAD_DIM
    z_col = (ATTN_WIDTH + 2 * KV_WIDTH) // gw
    meta_blk = (LEAD - N_META) // BF16_SUBLANES
    per_lead = BLOCK // BF16_SUBLANES

    def own(col0):
        return lambda b, h, pv, nx, ld, jb, nb: (b, col0 + h)

    def prev(col0):
        return lambda b, h, pv, nx, ld, jb, nb: (pv[b], col0 + h)

    def nxt(col0):
        return lambda b, h, pv, nx, ld, jb, nb: (nx[b], col0 + h)

    def meta(col0):
        return lambda b, h, pv, nx, ld, jb, nb: (ld[b] * per_lead + meta_blk, col0 + h)

    kv_blk = (BLOCK, HEAD_DIM)
    meta_kv = (N_META, HEAD_DIM)
    grid_spec = pltpu.PrefetchScalarGridSpec(
        num_scalar_prefetch=5,
        grid=(nblocks, N_KV_HEADS),
        in_specs=[
            pl.BlockSpec(memory_space=pltpu.SMEM),
            pl.BlockSpec((BLOCK, gw), own(0)),
            pl.BlockSpec(kv_blk, prev(k_col)),
            pl.BlockSpec(kv_blk, own(k_col)),
            pl.BlockSpec(kv_blk, nxt(k_col)),
            pl.BlockSpec(meta_kv, meta(k_col)),
            pl.BlockSpec(kv_blk, prev(v_col)),
            pl.BlockSpec(kv_blk, own(v_col)),
            pl.BlockSpec(kv_blk, nxt(v_col)),
            pl.BlockSpec(meta_kv, meta(v_col)),
            pl.BlockSpec((BLOCK, gw), own(z_col)),
        ],
        out_specs=pl.BlockSpec((BLOCK, gw), own(0)),
    )
    return pl.pallas_call(
        _attn_core_kernel,
        out_shape=jax.ShapeDtypeStruct((rows, ATTN_WIDTH), BF16),
        grid_spec=grid_spec,
        compiler_params=_params(2),
        name="attn_core",
    )(*tables, sink, *([proj] * 10))


def _finish(h_new, fw_ref):
    if fw_ref is None:
        return h_new
    ms = jnp.mean(h_new * h_new, axis=-1, keepdims=True)
    return (h_new * lax.rsqrt(ms + EPS)) * fw_ref[...]


def _attn_outproj_kernel(a_ref, w_ref, h_ref, *rest, final):
    fw_ref, o_ref = rest if final else (None, rest[0])
    h_new = h_ref[...] + jnp.dot(a_ref[...], w_ref[...], preferred_element_type=F32)
    o_ref[...] = _finish(h_new, fw_ref)


def _attn_outproj(a, w, h, fw=None):
    rows = h.shape[0]
    row_spec = pl.BlockSpec((ROW_TILE, D_MODEL), lambda i: (i, 0))
    in_specs = [row_spec, _resident((ATTN_WIDTH, D_MODEL)), row_spec]
    args = [a, w, h]
    if fw is not None:
        in_specs.append(_resident((1, D_MODEL)))
        args.append(fw)
    return pl.pallas_call(
        functools.partial(_attn_outproj_kernel, final=fw is not None),
        out_shape=jax.ShapeDtypeStruct((rows, D_MODEL), F32),
        grid=(rows // ROW_TILE,),
        in_specs=in_specs,
        out_specs=row_spec,
        compiler_params=_params(1),
        name="attn_outproj",
    )(*args)


def _conv_inproj_kernel(x_ref, g_ref, ok_ref, w_ref, cu_ref, t_ref, hn_ref):
    hn_ref[...] = _rmsnorm_bf16(x_ref[...], g_ref[...])
    ok = ok_ref[...] > 0.0
    for c in range(CONV_WIDTH // COL_CHUNK):
        def part(p):
            cols = slice(p * CONV_WIDTH + c * COL_CHUNK, p * CONV_WIDTH + (c + 1) * COL_CHUNK)
            return jnp.dot(hn_ref[...], w_ref[:, cols], preferred_element_type=F32)
        cols = slice(c * COL_CHUNK, (c + 1) * COL_CHUNK)
        cu_ref[:, cols] = jnp.where(ok, part(1) * part(2), 0.0).astype(BF16)
        t_ref[:, cols] = (part(0) * _silu(part(3))).astype(BF16)


def _conv_inproj(h, g, ok, w):
    rows = h.shape[0]
    row_spec = pl.BlockSpec((ROW_TILE, D_MODEL), lambda i: (i, 0))
    out = jax.ShapeDtypeStruct((rows, CONV_WIDTH), BF16)
    return pl.pallas_call(
        _conv_inproj_kernel,
        out_shape=(out, out),
        grid=(rows // ROW_TILE,),
        in_specs=[
            row_spec,
            _resident((1, D_MODEL)),
            pl.BlockSpec((ROW_TILE, 1), lambda i: (i, 0)),
            _resident((D_MODEL, 4 * CONV_WIDTH)),
        ],
        out_specs=(row_spec, row_spec),
        scratch_shapes=[pltpu.VMEM((ROW_TILE, D_MODEL), BF16)],
        compiler_params=_params(1),
        name="conv_inproj",
    )(h, g, ok, w)


_HALO = 8


def _conv_outproj_kernel(cu_ref, cup_ref, cun_ref, t_ref, cw_ref, w_ref, h_ref, *rest, final):
    if final:
        fw_ref, o_ref, pad_ref, g_ref = rest
    else:
        fw_ref = None
        o_ref, pad_ref, g_ref = rest
    i = pl.program_id(0)
    tm = ROW_TILE
    prev_row = jnp.where(i > 0, cup_ref[BF16_SUBLANES - 1:, :].astype(F32), 0.0)
    next_row = jnp.where(i < pl.num_programs(0) - 1, cun_ref[:1, :].astype(F32), 0.0)
    pad_ref[_HALO - 1:_HALO, :] = prev_row
    pad_ref[_HALO:_HALO + tm, :] = cu_ref[...].astype(F32)
    pad_ref[_HALO + tm:_HALO + tm + 1, :] = next_row
    y = (pad_ref[_HALO - 1:_HALO - 1 + tm, :] * cw_ref[0:1, :]
         + pad_ref[_HALO:_HALO + tm, :] * cw_ref[1:2, :]
         + pad_ref[_HALO + 1:_HALO + 1 + tm, :] * cw_ref[2:3, :])
    g_ref[...] = (t_ref[...].astype(F32) * y).astype(BF16)
    h_new = h_ref[...] + jnp.dot(g_ref[...], w_ref[...], preferred_element_type=F32)
    o_ref[...] = _finish(h_new, fw_ref)


def _conv_outproj(cu, t, cw, w, h, fw=None):
    rows = h.shape[0]
    per_tile = ROW_TILE // BF16_SUBLANES
    last = rows // BF16_SUBLANES - 1
    row_spec = pl.BlockSpec((ROW_TILE, D_MODEL), lambda i: (i, 0))
    halo = (BF16_SUBLANES, CONV_WIDTH)
    in_specs = [
        row_spec,
        pl.BlockSpec(halo, lambda i: (jnp.maximum(i * per_tile - 1, 0), 0)),
        pl.BlockSpec(halo, lambda i: (jnp.minimum((i + 1) * per_tile, last), 0)),
        row_spec,
        _resident((3, CONV_WIDTH)),
        _resident((CONV_WIDTH, D_MODEL)),
        row_spec,
    ]
    args = [cu, cu, cu, t, cw, w, h]
    if fw is not None:
        in_specs.append(_resident((1, D_MODEL)))
        args.append(fw)
    return pl.pallas_call(
        functools.partial(_conv_outproj_kernel, final=fw is not None),
        out_shape=jax.ShapeDtypeStruct((rows, D_MODEL), F32),
        grid=(rows // ROW_TILE,),
        in_specs=in_specs,
        out_specs=row_spec,
        scratch_shapes=[
            pltpu.VMEM((ROW_TILE + 2 * _HALO, CONV_WIDTH), F32),
            pltpu.VMEM((ROW_TILE, CONV_WIDTH), BF16),
        ],
        compiler_params=_params(1),
        name="conv_outproj",
    )(*args)


def _block_tables(seq_blocks):
    prev, nxt, lead, jblk, nblk = [], [], [], [], []
    start = 0
    for n in seq_blocks:
        for j in range(n):
            b = start + j
            prev.append(max(b - 1, start))
            nxt.append(min(b + 1, start + n - 1))
            lead.append(start)
            jblk.append(j)
            nblk.append(n)
        start += n
    return tuple(jnp.asarray(t, jnp.int32) for t in (prev, nxt, lead, jblk, nblk))


def _rope_tables(pos):
    half = HEAD_DIM // 2
    inv_freq = ROPE_THETA ** (-jnp.arange(0, half, dtype=F32) * (2.0 / HEAD_DIM))
    ang = pos.astype(F32)[:, None] * inv_freq[None, :]
    cos = jnp.concatenate([jnp.cos(ang), jnp.cos(ang)], -1)
    sin = jnp.concatenate([-jnp.sin(ang), jnp.sin(ang)], -1)
    return cos, sin


def kernel(x_prompt, x_sample, meta_tokens, norm_w, attn_w_in, attn_w_out, attn_sink,
           conv_w_in, conv_w, conv_w_out, final_norm_w):
    d = D_MODEL
    groups = (x_prompt, x_sample)
    lead = jnp.concatenate([jnp.zeros((LEAD - N_META, d), F32), meta_tokens.astype(F32)], axis=0)
    pieces, seq_blocks, pos = [], [], []
    for x in groups:
        bsz, s, _ = x.shape
        hx = jnp.concatenate([jnp.broadcast_to(lead[None], (bsz, LEAD, d)), x], axis=1)
        pieces.append(hx.reshape(bsz * (LEAD + s), d))
        seq_blocks += [(LEAD + s) // BLOCK] * bsz
        pos += [jnp.arange(LEAD + s, dtype=jnp.int32) - (LEAD - N_META)] * bsz
    h = jnp.concatenate(pieces, axis=0)
    pos = jnp.concatenate(pos)
    rows = h.shape[0]
    assert rows % ROW_TILE == 0

    cos, sin = _rope_tables(pos)
    ok = (pos >= 0).astype(F32)[:, None]
    tables = _block_tables(seq_blocks)

    depth = norm_w.shape[0]
    for i in range(depth):
        g = norm_w[i][None, :]
        fw = final_norm_w[None, :] if i == depth - 1 else None
        j = i // 2
        if i % 2 == 0:
            proj = _attn_inproj(h, g, attn_w_in[j].astype(BF16), cos, sin)
            a = _attn_core(proj, attn_sink[j], tables)
            h = _attn_outproj(a, attn_w_out[j].astype(BF16), h, fw)
        else:
            cu, t = _conv_inproj(h, g, ok, conv_w_in[j].astype(BF16))
            h = _conv_outproj(cu, t, conv_w[j], conv_w_out[j].astype(BF16), h, fw)

    outs, start = [], 0
    for x in groups:
        bsz, s, _ = x.shape
        n = bsz * (LEAD + s)
        outs.append(h[start:start + n].reshape(bsz, LEAD + s, d)[:, LEAD:])
        start += n
    return tuple(outs)
```

```python
import functools

import jax
import jax.numpy as jnp
from jax import lax
from jax.experimental import pallas as pl
from jax.experimental.pallas import tpu as pltpu

D_MODEL = 2048
HEAD_DIM = 128
N_Q_HEADS = 16
N_KV_HEADS = 4
GROUP = N_Q_HEADS // N_KV_HEADS
ATTN_WIDTH = N_Q_HEADS * HEAD_DIM
KV_WIDTH = N_KV_HEADS * HEAD_DIM
ATTN_IN = 2 * ATTN_WIDTH + 2 * KV_WIDTH
CONV_WIDTH = D_MODEL
WINDOW = 128
BLOCK = 128
N_META = 16
LEAD = BLOCK
ROPE_THETA = 10000.0
EPS = 1e-6
NEG = -1e30
LOG2E = 1.4426950408889634

ROW_TILE = 384
COL_CHUNK = 512
BF16_SUBLANES = 16
VMEM_LIMIT = 58 * 1024 * 1024

Z_COL0 = ATTN_WIDTH
K_COL0 = 2 * ATTN_WIDTH
V_COL0 = 2 * ATTN_WIDTH + KV_WIDTH

F32 = jnp.float32
BF16 = jnp.bfloat16


def _params(n_axes):
    return pltpu.CompilerParams(
        dimension_semantics=("arbitrary",) * n_axes, vmem_limit_bytes=VMEM_LIMIT)


def _resident(shape):
    return pl.BlockSpec(shape, lambda *_: (0,) * len(shape), pipeline_mode=pl.Buffered(1))


def _rmsnorm_bf16(x, g):
    ms = jnp.mean(x * x, axis=-1, keepdims=True)
    return ((x * lax.rsqrt(ms + EPS)) * g).astype(BF16)


def _silu(z):
    return z * jax.nn.sigmoid(z)


def _attn_inproj_kernel(x_ref, g_ref, w_ref, cos_ref, sin_ref, o_ref, hn_ref):
    hn_ref[...] = _rmsnorm_bf16(x_ref[...], g_ref[...])
    cos = cos_ref[...]
    sin = sin_ref[...]
    for c in range(ATTN_IN // COL_CHUNK):
        cols = slice(c * COL_CHUNK, (c + 1) * COL_CHUNK)
        acc = jnp.dot(hn_ref[...], w_ref[:, cols], preferred_element_type=F32)
        col0 = c * COL_CHUNK
        if col0 < Z_COL0 or K_COL0 <= col0 < V_COL0:
            heads = []
            for hh in range(COL_CHUNK // HEAD_DIM):
                xh = acc[:, hh * HEAD_DIM:(hh + 1) * HEAD_DIM]
                heads.append(xh * cos + pltpu.roll(xh, HEAD_DIM // 2, axis=1) * sin)
            acc = jnp.concatenate(heads, axis=1)
        o_ref[:, cols] = acc.astype(BF16)


def _attn_inproj(h, g, w, cos, sin):
    rows = h.shape[0]
    return pl.pallas_call(
        _attn_inproj_kernel,
        out_shape=jax.ShapeDtypeStruct((rows, ATTN_IN), BF16),
        grid=(rows // ROW_TILE,),
        in_specs=[
            pl.BlockSpec((ROW_TILE, D_MODEL), lambda i: (i, 0)),
            _resident((1, D_MODEL)),
            _resident((D_MODEL, ATTN_IN)),
            pl.BlockSpec((ROW_TILE, HEAD_DIM), lambda i: (i, 0)),
            pl.BlockSpec((ROW_TILE, HEAD_DIM), lambda i: (i, 0)),
        ],
        out_specs=pl.BlockSpec((ROW_TILE, ATTN_IN), lambda i: (i, 0)),
        scratch_shapes=[pltpu.VMEM((ROW_TILE, D_MODEL), BF16)],
        compiler_params=_params(1),
        name="attn_inproj",
    )(h, g, w, cos, sin)


def _attn_core_kernel(prev_ref, next_ref, lead_ref, jblk_ref, nblk_ref, sink_ref,
                      q_ref, kp_ref, ko_ref, kn_ref, km_ref,
                      vp_ref, vo_ref, vn_ref, vm_ref, z_ref, o_ref):
    b = pl.program_id(0)
    j = jblk_ref[b]
    n = nblk_ref[b]
    nk = 3 * BLOCK + N_META

    r = lax.broadcasted_iota(jnp.int32, (BLOCK, nk), 0)
    c = lax.broadcasted_iota(jnp.int32, (BLOCK, nk), 1)
    part = c // BLOCK
    cc = c % BLOCK
    pos_q = j * BLOCK + r - (LEAD - N_META)
    pos_k = (j - 1 + part) * BLOCK + cc - (LEAD - N_META)
    kblk = j - 1 + part
    band = ((kblk >= 0) & (kblk < n) & (pos_k >= 0)
            & (jnp.abs(pos_q - pos_k) <= WINDOW))
    meta = jnp.abs(pos_q - cc) > WINDOW
    mask = ((part == 3) & meta) | ((part < 3) & band)

    scale = HEAD_DIM ** -0.5
    c_exp = scale * LOG2E
    for h in range(N_KV_HEADS):
        kv = slice(h * HEAD_DIM, (h + 1) * HEAD_DIM)
        kcat = jnp.concatenate([kp_ref[:, kv], ko_ref[:, kv], kn_ref[:, kv], km_ref[:, kv]], axis=0)
        vcat = jnp.concatenate([vp_ref[:, kv], vo_ref[:, kv], vn_ref[:, kv], vm_ref[:, kv]], axis=0)
        heads = [slice((h * GROUP + g) * HEAD_DIM, (h * GROUP + g + 1) * HEAD_DIM)
                 for g in range(GROUP)]
        qs = jnp.concatenate([q_ref[:, cols] for cols in heads], axis=0)
        s_all = lax.dot_general(qs, kcat, (((1,), (1,)), ((), ())), preferred_element_type=F32)
        probs = []
        for g in range(GROUP):
            s = jnp.where(mask, s_all[g * BLOCK:(g + 1) * BLOCK], NEG)
            sink = sink_ref[h * GROUP + g] * (1.0 / scale)
            m = jnp.maximum(jnp.max(s, axis=-1, keepdims=True), sink)
            e = jnp.exp2((s - m) * c_exp)
            denom = jnp.sum(e, axis=-1, keepdims=True) + jnp.exp2((sink - m) * c_exp)
            probs.append((e * (1.0 / denom)).astype(BF16))
        o_all = jnp.dot(jnp.concatenate(probs, axis=0), vcat, preferred_element_type=F32)
        for g, cols in enumerate(heads):
            z = z_ref[:, cols].astype(F32)
            o_ref[:, cols] = (o_all[g * BLOCK:(g + 1) * BLOCK] * _silu(z)).astype(BF16)


def _attn_core(proj, sink, tables):
    rows = proj.shape[0]
    nblocks = rows // BLOCK
    z_col = Z_COL0 // ATTN_WIDTH
    k_col = K_COL0 // KV_WIDTH
    v_col = V_COL0 // KV_WIDTH
    meta_blk = (LEAD - N_META) // BF16_SUBLANES
    per_lead = BLOCK // BF16_SUBLANES

    def own(col):
        return lambda b, pv, nx, ld, jb, nb: (b, col)

    def prev(col):
        return lambda b, pv, nx, ld, jb, nb: (pv[b], col)

    def nxt(col):
        return lambda b, pv, nx, ld, jb, nb: (nx[b], col)

    def meta(col):
        return lambda b, pv, nx, ld, jb, nb: (ld[b] * per_lead + meta_blk, col)

    kv_blk = (BLOCK, KV_WIDTH)
    meta_kv = (N_META, KV_WIDTH)
    grid_spec = pltpu.PrefetchScalarGridSpec(
        num_scalar_prefetch=5,
        grid=(nblocks,),
        in_specs=[
            pl.BlockSpec(memory_space=pltpu.SMEM),
            pl.BlockSpec((BLOCK, ATTN_WIDTH), own(0)),
            pl.BlockSpec(kv_blk, prev(k_col)),
            pl.BlockSpec(kv_blk, own(k_col)),
            pl.BlockSpec(kv_blk, nxt(k_col)),
            pl.BlockSpec(meta_kv, meta(k_col)),
            pl.BlockSpec(kv_blk, prev(v_col)),
            pl.BlockSpec(kv_blk, own(v_col)),
            pl.BlockSpec(kv_blk, nxt(v_col)),
            pl.BlockSpec(meta_kv, meta(v_col)),
            pl.BlockSpec((BLOCK, ATTN_WIDTH), own(z_col)),
        ],
        out_specs=pl.BlockSpec((BLOCK, ATTN_WIDTH), own(0)),
    )
    return pl.pallas_call(
        _attn_core_kernel,
        out_shape=jax.ShapeDtypeStruct((rows, ATTN_WIDTH), BF16),
        grid_spec=grid_spec,
        compiler_params=_params(1),
        name="attn_core",
    )(*tables, sink, *([proj] * 10))


def _finish(h_new, fw_ref):
    if fw_ref is None:
        return h_new
    ms = jnp.mean(h_new * h_new, axis=-1, keepdims=True)
    return (h_new * lax.rsqrt(ms + EPS)) * fw_ref[...]


def _attn_outproj_kernel(a_ref, w_ref, h_ref, *rest, final):
    fw_ref, o_ref = rest if final else (None, rest[0])
    h_new = h_ref[...] + jnp.dot(a_ref[...], w_ref[...], preferred_element_type=F32)
    o_ref[...] = _finish(h_new, fw_ref)


def _attn_outproj(a, w, h, fw=None):
    rows = h.shape[0]
    row_spec = pl.BlockSpec((ROW_TILE, D_MODEL), lambda i: (i, 0))
    in_specs = [row_spec, _resident((ATTN_WIDTH, D_MODEL)), row_spec]
    args = [a, w, h]
    if fw is not None:
        in_specs.append(_resident((1, D_MODEL)))
        args.append(fw)
    return pl.pallas_call(
        functools.partial(_attn_outproj_kernel, final=fw is not None),
        out_shape=jax.ShapeDtypeStruct((rows, D_MODEL), F32),
        grid=(rows // ROW_TILE,),
        in_specs=in_specs,
        out_specs=row_spec,
        compiler_params=_params(1),
        name="attn_outproj",
    )(*args)


def _conv_inproj_kernel(x_ref, g_ref, ok_ref, w_ref, cu_ref, t_ref, hn_ref):
    hn_ref[...] = _rmsnorm_bf16(x_ref[...], g_ref[...])
    ok = ok_ref[...] > 0.0
    for c in range(CONV_WIDTH // COL_CHUNK):
        def part(p):
            cols = slice(p * CONV_WIDTH + c * COL_CHUNK, p * CONV_WIDTH + (c + 1) * COL_CHUNK)
            return jnp.dot(hn_ref[...], w_ref[:, cols], preferred_element_type=F32)
        cols = slice(c * COL_CHUNK, (c + 1) * COL_CHUNK)
        cu_ref[:, cols] = jnp.where(ok, part(1) * part(2), 0.0).astype(BF16)
        t_ref[:, cols] = (part(0) * _silu(part(3))).astype(BF16)


def _conv_inproj(h, g, ok, w):
    rows = h.shape[0]
    row_spec = pl.BlockSpec((ROW_TILE, D_MODEL), lambda i: (i, 0))
    out = jax.ShapeDtypeStruct((rows, CONV_WIDTH), BF16)
    return pl.pallas_call(
        _conv_inproj_kernel,
        out_shape=(out, out),
        grid=(rows // ROW_TILE,),
        in_specs=[
            row_spec,
            _resident((1, D_MODEL)),
            pl.BlockSpec((ROW_TILE, 1), lambda i: (i, 0)),
            _resident((D_MODEL, 4 * CONV_WIDTH)),
        ],
        out_specs=(row_spec, row_spec),
        scratch_shapes=[pltpu.VMEM((ROW_TILE, D_MODEL), BF16)],
        compiler_params=_params(1),
        name="conv_inproj",
    )(h, g, ok, w)


_HALO = 8


def _conv_outproj_kernel(cu_ref, cup_ref, cun_ref, t_ref, cw_ref, w_ref, h_ref, *rest, final):
    if final:
        fw_ref, o_ref, pad_ref, g_ref = rest
    else:
        fw_ref = None
        o_ref, pad_ref, g_ref = rest
    i = pl.program_id(0)
    tm = ROW_TILE
    prev_row = jnp.where(i > 0, cup_ref[BF16_SUBLANES - 1:, :].astype(F32), 0.0)
    next_row = jnp.where(i < pl.num_programs(0) - 1, cun_ref[:1, :].astype(F32), 0.0)
    pad_ref[_HALO - 1:_HALO, :] = prev_row
    pad_ref[_HALO:_HALO + tm, :] = cu_ref[...].astype(F32)
    pad_ref[_HALO + tm:_HALO + tm + 1, :] = next_row
    y = (pad_ref[_HALO - 1:_HALO - 1 + tm, :] * cw_ref[0:1, :]
         + pad_ref[_HALO:_HALO + tm, :] * cw_ref[1:2, :]
         + pad_ref[_HALO + 1:_HALO + 1 + tm, :] * cw_ref[2:3, :])
    g_ref[...] = (t_ref[...].astype(F32) * y).astype(BF16)
    h_new = h_ref[...] + jnp.dot(g_ref[...], w_ref[...], preferred_element_type=F32)
    o_ref[...] = _finish(h_new, fw_ref)


def _conv_outproj(cu, t, cw, w, h, fw=None):
    rows = h.shape[0]
    per_tile = ROW_TILE // BF16_SUBLANES
    last = rows // BF16_SUBLANES - 1
    row_spec = pl.BlockSpec((ROW_TILE, D_MODEL), lambda i: (i, 0))
    halo = (BF16_SUBLANES, CONV_WIDTH)
    in_specs = [
        row_spec,
        pl.BlockSpec(halo, lambda i: (jnp.maximum(i * per_tile - 1, 0), 0)),
        pl.BlockSpec(halo, lambda i: (jnp.minimum((i + 1) * per_tile, last), 0)),
        row_spec,
        _resident((3, CONV_WIDTH)),
        _resident((CONV_WIDTH, D_MODEL)),
        row_spec,
    ]
    args = [cu, cu, cu, t, cw, w, h]
    if fw is not None:
        in_specs.append(_resident((1, D_MODEL)))
        args.append(fw)
    return pl.pallas_call(
        functools.partial(_conv_outproj_kernel, final=fw is not None),
        out_shape=jax.ShapeDtypeStruct((rows, D_MODEL), F32),
        grid=(rows // ROW_TILE,),
        in_specs=in_specs,
        out_specs=row_spec,
        scratch_shapes=[
            pltpu.VMEM((ROW_TILE + 2 * _HALO, CONV_WIDTH), F32),
            pltpu.VMEM((ROW_TILE, CONV_WIDTH), BF16),
        ],
        compiler_params=_params(1),
        name="conv_outproj",
    )(*args)


def _block_tables(seq_blocks):
    prev, nxt, lead, jblk, nblk = [], [], [], [], []
    start = 0
    for n in seq_blocks:
        for j in range(n):
            b = start + j
            prev.append(max(b - 1, start))
            nxt.append(min(b + 1, start + n - 1))
            lead.append(start)
            jblk.append(j)
            nblk.append(n)
        start += n
    return tuple(jnp.asarray(t, jnp.int32) for t in (prev, nxt, lead, jblk, nblk))


def _rope_tables(pos):
    half = HEAD_DIM // 2
    inv_freq = ROPE_THETA ** (-jnp.arange(0, half, dtype=F32) * (2.0 / HEAD_DIM))
    ang = pos.astype(F32)[:, None] * inv_freq[None, :]
    cos = jnp.concatenate([jnp.cos(ang), jnp.cos(ang)], -1)
    sin = jnp.concatenate([-jnp.sin(ang), jnp.sin(ang)], -1)
    return cos, sin


def _attn_w_in_bf16(w):
    kv0, z0 = ATTN_WIDTH, ATTN_WIDTH + 2 * KV_WIDTH
    return jnp.concatenate([w[:, :kv0], w[:, z0:], w[:, kv0:z0]], axis=1).astype(BF16)


def kernel(x_prompt, x_sample, meta_tokens, norm_w, attn_w_in, attn_w_out, attn_sink,
           conv_w_in, conv_w, conv_w_out, final_norm_w):
    d = D_MODEL
    groups = (x_prompt, x_sample)
    lead = jnp.concatenate([jnp.zeros((LEAD - N_META, d), F32), meta_tokens.astype(F32)], axis=0)
    pieces, seq_blocks, pos = [], [], []
    for x in groups:
        bsz, s, _ = x.shape
        hx = jnp.concatenate([jnp.broadcast_to(lead[None], (bsz, LEAD, d)), x], axis=1)
        pieces.append(hx.reshape(bsz * (LEAD + s), d))
        seq_blocks += [(LEAD + s) // BLOCK] * bsz
        pos += [jnp.arange(LEAD + s, dtype=jnp.int32) - (LEAD - N_META)] * bsz
    h = jnp.concatenate(pieces, axis=0)
    pos = jnp.concatenate(pos)
    rows = h.shape[0]
    assert rows % ROW_TILE == 0

    cos, sin = _rope_tables(pos)
    ok = (pos >= 0).astype(F32)[:, None]
    tables = _block_tables(seq_blocks)

    depth = norm_w.shape[0]
    for i in range(depth):
        g = norm_w[i][None, :]
        fw = final_norm_w[None, :] if i == depth - 1 else None
        j = i // 2
        if i % 2 == 0:
            proj = _attn_inproj(h, g, _attn_w_in_bf16(attn_w_in[j]), cos, sin)
            a = _attn_core(proj, attn_sink[j], tables)
            h = _attn_outproj(a, attn_w_out[j].astype(BF16), h, fw)
        else:
            cu, t = _conv_inproj(h, g, ok, conv_w_in[j].astype(BF16))
            h = _conv_outproj(cu, t, conv_w[j], conv_w_out[j].astype(BF16), h, fw)

    outs, start = [], 0
    for x in groups:
        bsz, s, _ = x.shape
        n = bsz * (LEAD + s)
        outs.append(h[start:start + n].reshape(bsz, LEAD + s, d)[:, LEAD:])
        start += n
    return tuple(outs)
```

```python
import functools

import jax
import jax.numpy as jnp
import numpy as np
from jax import lax
from jax.experimental import pallas as pl
from jax.experimental.pallas import tpu as pltpu

D_MODEL = 2048
HEAD_DIM = 128
N_Q_HEADS = 16
N_KV_HEADS = 4
GROUP = N_Q_HEADS // N_KV_HEADS
ATTN_WIDTH = N_Q_HEADS * HEAD_DIM
KV_WIDTH = N_KV_HEADS * HEAD_DIM
ATTN_IN = 2 * ATTN_WIDTH + 2 * KV_WIDTH
CONV_WIDTH = D_MODEL
WINDOW = 128
BLOCK = 128
N_META = 16
LEAD = BLOCK
ROPE_THETA = 10000.0
EPS = 1e-6
NEG = -1e30
LOG2E = 1.4426950408889634

ROW_TILE = 512
CONV_IN_TILE = 256
FIRST_IN_TILE = 256
COL_CHUNK = 512
ROW_CHUNK = 256
Q_SCALE = HEAD_DIM ** -0.5 * LOG2E
BF16_SUBLANES = 16
LEADS_PER_TILE = ROW_TILE // BLOCK
VMEM_LIMIT = 58 * 1024 * 1024

Z_COL0 = ATTN_WIDTH
K_COL0 = 2 * ATTN_WIDTH
V_COL0 = 2 * ATTN_WIDTH + KV_WIDTH

F32 = jnp.float32
BF16 = jnp.bfloat16


def _params(n_axes):
    return pltpu.CompilerParams(
        dimension_semantics=("arbitrary",) * n_axes, vmem_limit_bytes=VMEM_LIMIT)


def _resident(shape):
    return pl.BlockSpec(shape, lambda *_: (0,) * len(shape), pipeline_mode=pl.Buffered(1))


def _rmsnorm_bf16(x, g):
    ms = jnp.mean(x * x, axis=-1, keepdims=True)
    return ((x * lax.rsqrt(ms + EPS)) * g).astype(BF16)


def _silu(z):
    return z * jax.nn.sigmoid(z)


def _key_bias(j, n):
    nk = 3 * BLOCK + N_META
    r = np.arange(BLOCK)[:, None]
    c = np.arange(nk)[None, :]
    part, cc = c // BLOCK, c % BLOCK
    pos_q = j * BLOCK + r - (LEAD - N_META)
    kblk = j - 1 + part
    pos_k = kblk * BLOCK + cc - (LEAD - N_META)
    band = (kblk >= 0) & (kblk < n) & (pos_k >= 0) & (np.abs(pos_q - pos_k) <= WINDOW)
    meta = np.abs(pos_q - cc) > WINDOW
    visible = np.where(part == 3, meta, band)
    return np.where(visible, 0.0, NEG).astype(np.float32)


class Layout:
    def __init__(self, group_shapes):
        self.seq_rows = []
        self.group_rows = []
        for bsz, s in group_shapes:
            assert s % ROW_TILE == 0
            self.seq_rows += [s] * bsz
            self.group_rows.append(bsz * s)
        self.n_seq = len(self.seq_rows)
        self.seq_row0 = np.concatenate([[0], np.cumsum(self.seq_rows)[:-1]]).astype(int)
        self.main_rows = int(sum(self.seq_rows))
        self.n_lead = -(-self.n_seq // LEADS_PER_TILE) * LEADS_PER_TILE
        self.rows = self.main_rows + self.n_lead * LEAD
        self.main_tiles = self.main_rows // ROW_TILE
        self.tiles = self.rows // ROW_TILE
        self.main_blocks = self.main_rows // BLOCK
        self.blocks = self.rows // BLOCK

    def lead_row0(self, s):
        return self.main_rows + s * LEAD

    def block_tables(self):
        prev, nxt, lead, case = (np.zeros(self.blocks, np.int32) for _ in range(4))
        biases = []

        def bias_case(j, n):
            bias = _key_bias(j, n)
            for k, known in enumerate(biases):
                if np.array_equal(known, bias):
                    return k
            biases.append(bias)
            return len(biases) - 1

        for s in range(self.n_lead):
            lb = self.main_blocks + s
            real = s < self.n_seq
            first = self.seq_row0[s] // BLOCK if real else lb
            n = self.seq_rows[s] // BLOCK + 1 if real else 1
            prev[lb], nxt[lb], lead[lb], case[lb] = lb, first, lb, bias_case(0, n)
            for j in range(1, n):
                b = first + j - 1
                prev[b] = lb if j == 1 else b - 1
                nxt[b] = b + 1 if j < n - 1 else b
                lead[b], case[b] = lb, bias_case(j, n)
        tables = tuple(jnp.asarray(t) for t in (prev, nxt, lead, case))
        return tables, jnp.asarray(np.stack(biases))

    def conv_tables(self):
        per = ROW_TILE // BF16_SUBLANES
        prev = np.zeros(self.tiles, np.int32)
        nxt = np.zeros(self.tiles, np.int32)
        zero_next = np.zeros(self.tiles, np.int32)
        is_lead = np.zeros(self.tiles, np.int32)
        first = np.zeros((LEADS_PER_TILE, self.tiles), np.int32)
        starts = {int(r): s for s, r in enumerate(self.seq_row0)}
        ends = {int(r + n) for r, n in zip(self.seq_row0, self.seq_rows)}
        for i in range(self.tiles):
            row0 = i * ROW_TILE
            nxt[i] = min((i + 1) * per, self.rows // BF16_SUBLANES - 1)
            if i < self.main_tiles:
                if row0 in starts:
                    prev[i] = (self.lead_row0(starts[row0]) + LEAD) // BF16_SUBLANES - 1
                else:
                    prev[i] = i * per - 1
                zero_next[i] = int(row0 + ROW_TILE in ends)
            else:
                prev[i] = i * per - 1
                is_lead[i] = 1
                for k in range(LEADS_PER_TILE):
                    s = (i - self.main_tiles) * LEADS_PER_TILE + k
                    first[k, i] = self.seq_row0[s] // BF16_SUBLANES if s < self.n_seq else 0
        return tuple(jnp.asarray(t) for t in (prev, nxt, zero_next, is_lead, *first))

    def rope_tables(self, tm):
        longest = max(self.seq_rows)
        pos_main = np.arange(longest) + N_META
        pos_lead = np.tile(np.arange(LEAD) - (LEAD - N_META), tm // LEAD)
        pos = jnp.asarray(np.concatenate([pos_main, pos_lead]), jnp.int32)
        half = HEAD_DIM // 2
        inv_freq = ROPE_THETA ** (-jnp.arange(0, half, dtype=F32) * (2.0 / HEAD_DIM))
        ang = pos.astype(F32)[:, None] * inv_freq[None, :]
        cos = jnp.concatenate([jnp.cos(ang), jnp.cos(ang)], -1)
        sin = jnp.concatenate([-jnp.sin(ang), jnp.sin(ang)], -1)
        tile = np.zeros(self.rows // tm, np.int32)
        for s in range(self.n_seq):
            t0 = self.seq_row0[s] // tm
            for k in range(self.seq_rows[s] // tm):
                tile[t0 + k] = k
        tile[self.main_rows // tm:] = longest // tm
        return cos, sin, jnp.asarray(tile)


def _attn_inproj_kernel(tile_ref, *refs, src_tiles):
    n_src = len(src_tiles)
    x_refs = refs[:n_src]
    g_ref, w_ref, cos_ref, sin_ref, o_ref = refs[n_src:n_src + 5]
    rest = refs[n_src + 5:]
    h_ref, hn_ref = rest if n_src > 1 else (None, rest[0])
    i = pl.program_id(0)

    tm = hn_ref.shape[0]
    if n_src > 1:
        t0 = 0
        for x_ref, nt in zip(x_refs, src_tiles):
            @pl.when((i >= t0) & (i < t0 + nt))
            def _(x_ref=x_ref):
                x = x_ref[...]
                hn_ref[...] = _rmsnorm_bf16(x, g_ref[...])
                h_ref[...] = x
            t0 += nt

    n_chunks = 1 if n_src > 1 else max(1, tm // ROW_CHUNK)
    rows_per = tm // n_chunks
    for rc in range(n_chunks):
        rows = slice(rc * rows_per, (rc + 1) * rows_per)
        if n_src == 1:
            hn_ref[rows, :] = _rmsnorm_bf16(x_refs[0][rows, :], g_ref[...])
        cos = cos_ref[rows, :]
        sin = sin_ref[rows, :]
        cos_q = cos * Q_SCALE
        sin_q = sin * Q_SCALE
        for c in range(ATTN_IN // COL_CHUNK):
            cols = slice(c * COL_CHUNK, (c + 1) * COL_CHUNK)
            acc = jnp.dot(hn_ref[rows, :], w_ref[:, cols], preferred_element_type=F32)
            col0 = c * COL_CHUNK
            is_q = col0 < Z_COL0
            if is_q or K_COL0 <= col0 < V_COL0:
                cs, sn = (cos_q, sin_q) if is_q else (cos, sin)
                heads = []
                for hh in range(COL_CHUNK // HEAD_DIM):
                    xh = acc[:, hh * HEAD_DIM:(hh + 1) * HEAD_DIM]
                    heads.append(xh * cs + pltpu.roll(xh, HEAD_DIM // 2, axis=1) * sn)
                acc = jnp.concatenate(heads, axis=1)
            elif Z_COL0 <= col0 < K_COL0:
                acc = _silu(acc)
            o_ref[rows, cols] = acc.astype(BF16)


def _attn_inproj(lay, srcs, g, w, tm):
    cos, sin, rope_tile = lay.rope_tables(tm)
    src_tiles = [x.shape[0] // tm for x in srcs]
    row_spec = pl.BlockSpec((tm, D_MODEL), lambda i, rt: (i, 0))
    x_specs, t0 = [], 0
    for nt in src_tiles:
        x_specs.append(pl.BlockSpec(
            (tm, D_MODEL), lambda i, rt, t0=t0, nt=nt: (jnp.clip(i - t0, 0, nt - 1), 0)))
        t0 += nt
    assert t0 * tm == lay.rows
    rope_spec = pl.BlockSpec((tm, HEAD_DIM), lambda i, rt: (rt[i], 0))
    proj = jax.ShapeDtypeStruct((lay.rows, ATTN_IN), BF16)
    proj_spec = pl.BlockSpec((tm, ATTN_IN), lambda i, rt: (i, 0))
    first = len(srcs) > 1
    grid_spec = pltpu.PrefetchScalarGridSpec(
        num_scalar_prefetch=1,
        grid=(lay.rows // tm,),
        in_specs=x_specs + [_resident((1, D_MODEL)), _resident((D_MODEL, ATTN_IN)),
                            rope_spec, rope_spec],
        out_specs=(proj_spec, row_spec) if first else proj_spec,
        scratch_shapes=[pltpu.VMEM((tm, D_MODEL), BF16)],
    )
    h_out = jax.ShapeDtypeStruct((lay.rows, D_MODEL), F32)
    return pl.pallas_call(
        functools.partial(_attn_inproj_kernel, src_tiles=tuple(src_tiles)),
        out_shape=(proj, h_out) if first else proj,
        grid_spec=grid_spec,
        compiler_params=_params(1),
        name="attn_inproj",
    )(rope_tile, *srcs, g, w, cos, sin)


def _attn_core_kernel(prev_ref, next_ref, lead_ref, case_ref, sink_ref, bias_ref,
                      q_ref, kp_ref, ko_ref, kn_ref, km_ref,
                      vp_ref, vo_ref, vn_ref, vm_ref, gate_ref, o_ref):
    bias = bias_ref[0]
    for h in range(N_KV_HEADS):
        kv = slice(h * HEAD_DIM, (h + 1) * HEAD_DIM)
        kcat = jnp.concatenate([kp_ref[:, kv], ko_ref[:, kv], kn_ref[:, kv], km_ref[:, kv]], axis=0)
        vcat = jnp.concatenate([vp_ref[:, kv], vo_ref[:, kv], vn_ref[:, kv], vm_ref[:, kv]], axis=0)
        heads = [slice((h * GROUP + g) * HEAD_DIM, (h * GROUP + g + 1) * HEAD_DIM)
                 for g in range(GROUP)]
        qs = jnp.concatenate([q_ref[:, cols] for cols in heads], axis=0)
        s_all = lax.dot_general(qs, kcat, (((1,), (1,)), ((), ())), preferred_element_type=F32)
        probs, inv = [], []
        for g in range(GROUP):
            s = s_all[g * BLOCK:(g + 1) * BLOCK] + bias
            sink = sink_ref[h * GROUP + g] * LOG2E
            m = jnp.maximum(jnp.max(s, axis=-1, keepdims=True), sink)
            e = jnp.exp2(s - m)
            inv.append(1.0 / (jnp.sum(e, axis=-1, keepdims=True) + jnp.exp2(sink - m)))
            probs.append(e.astype(BF16))
        o_all = jnp.dot(jnp.concatenate(probs, axis=0), vcat, preferred_element_type=F32)
        for g, cols in enumerate(heads):
            o = o_all[g * BLOCK:(g + 1) * BLOCK] * inv[g]
            o_ref[:, cols] = (o * gate_ref[:, cols].astype(F32)).astype(BF16)


def _attn_core(lay, proj, sink, tables, biases):
    z_col = Z_COL0 // ATTN_WIDTH
    k_col = K_COL0 // KV_WIDTH
    v_col = V_COL0 // KV_WIDTH
    meta_blk = (LEAD - N_META) // BF16_SUBLANES
    per_lead = BLOCK // BF16_SUBLANES

    def own(col):
        return lambda b, pv, nx, ld, cs: (b, col)

    def prev(col):
        return lambda b, pv, nx, ld, cs: (pv[b], col)

    def nxt(col):
        return lambda b, pv, nx, ld, cs: (nx[b], col)

    def meta(col):
        return lambda b, pv, nx, ld, cs: (ld[b] * per_lead + meta_blk, col)

    kv_blk = (BLOCK, KV_WIDTH)
    meta_kv = (N_META, KV_WIDTH)
    grid_spec = pltpu.PrefetchScalarGridSpec(
        num_scalar_prefetch=4,
        grid=(lay.blocks,),
        in_specs=[
            pl.BlockSpec(memory_space=pltpu.SMEM),
            pl.BlockSpec((1,) + biases.shape[1:], lambda b, pv, nx, ld, cs: (cs[b], 0, 0)),
            pl.BlockSpec((BLOCK, ATTN_WIDTH), own(0)),
            pl.BlockSpec(kv_blk, prev(k_col)),
            pl.BlockSpec(kv_blk, own(k_col)),
            pl.BlockSpec(kv_blk, nxt(k_col)),
            pl.BlockSpec(meta_kv, meta(k_col)),
            pl.BlockSpec(kv_blk, prev(v_col)),
            pl.BlockSpec(kv_blk, own(v_col)),
            pl.BlockSpec(kv_blk, nxt(v_col)),
            pl.BlockSpec(meta_kv, meta(v_col)),
            pl.BlockSpec((BLOCK, ATTN_WIDTH), own(z_col)),
        ],
        out_specs=pl.BlockSpec((BLOCK, ATTN_WIDTH), own(0)),
    )
    return pl.pallas_call(
        _attn_core_kernel,
        out_shape=jax.ShapeDtypeStruct((lay.rows, ATTN_WIDTH), BF16),
        grid_spec=grid_spec,
        compiler_params=_params(1),
        name="attn_core",
    )(*tables, sink, biases, *([proj] * 10))


def _finish(h_new, fw_ref):
    if fw_ref is None:
        return h_new
    ms = jnp.mean(h_new * h_new, axis=-1, keepdims=True)
    return (h_new * lax.rsqrt(ms + EPS)) * fw_ref[...]


def _attn_outproj_kernel(a_ref, w_ref, h_ref, *rest, final):
    fw_ref, o_ref = rest if final else (None, rest[0])
    h_new = h_ref[...] + jnp.dot(a_ref[...], w_ref[...], preferred_element_type=F32)
    o_ref[...] = _finish(h_new, fw_ref)


def _attn_outproj(a, w, h, tile0, n_tiles, fw=None):
    in_rows = pl.BlockSpec((ROW_TILE, D_MODEL), lambda i: (i + tile0, 0))
    in_specs = [in_rows, _resident((ATTN_WIDTH, D_MODEL)), in_rows]
    args = [a, w, h]
    if fw is not None:
        in_specs.append(_resident((1, D_MODEL)))
        args.append(fw)
    return pl.pallas_call(
        functools.partial(_attn_outproj_kernel, final=fw is not None),
        out_shape=jax.ShapeDtypeStruct((n_tiles * ROW_TILE, D_MODEL), F32),
        grid=(n_tiles,),
        in_specs=in_specs,
        out_specs=pl.BlockSpec((ROW_TILE, D_MODEL), lambda i: (i, 0)),
        compiler_params=_params(1),
        name="attn_outproj",
    )(*args)


def _conv_inproj_kernel(x_ref, g_ref, w_ref, cu_ref, t_ref, hn_ref, *, main_tiles):
    tm = CONV_IN_TILE
    hn_ref[...] = _rmsnorm_bf16(x_ref[...], g_ref[...])
    r = lax.broadcasted_iota(jnp.int32, (tm, 1), 0)
    first_ok = jnp.where(pl.program_id(0) < main_tiles, 0, LEAD - N_META)
    ok = r % BLOCK >= first_ok
    for c in range(CONV_WIDTH // COL_CHUNK):
        def part(p):
            cols = slice(p * CONV_WIDTH + c * COL_CHUNK, p * CONV_WIDTH + (c + 1) * COL_CHUNK)
            return jnp.dot(hn_ref[...], w_ref[:, cols], preferred_element_type=F32)
        cols = slice(c * COL_CHUNK, (c + 1) * COL_CHUNK)
        cu_ref[:, cols] = jnp.where(ok, part(1) * part(2), 0.0).astype(BF16)
        t_ref[:, cols] = (part(0) * _silu(part(3))).astype(BF16)


def _conv_inproj(lay, h, g, w):
    tm = CONV_IN_TILE
    row_spec = pl.BlockSpec((tm, D_MODEL), lambda i: (i, 0))
    out = jax.ShapeDtypeStruct((lay.rows, CONV_WIDTH), BF16)
    return pl.pallas_call(
        functools.partial(_conv_inproj_kernel, main_tiles=lay.main_rows // tm),
        out_shape=(out, out),
        grid=(lay.rows // tm,),
        in_specs=[row_spec, _resident((1, D_MODEL)), _resident((D_MODEL, 4 * CONV_WIDTH))],
        out_specs=(row_spec, row_spec),
        scratch_shapes=[pltpu.VMEM((tm, D_MODEL), BF16)],
        compiler_params=_params(1),
        name="conv_inproj",
    )(h, g, w)


_HALO = 8


def _conv_outproj_kernel(prev_ref, next_ref, zero_next_ref, is_lead_ref, *refs, tile0, final):
    first_tbl = refs[:LEADS_PER_TILE]
    refs = refs[LEADS_PER_TILE:]
    cu_ref, cup_ref, cun_ref = refs[:3]
    first_refs = refs[3:3 + LEADS_PER_TILE]
    t_ref, cw_ref, w_ref, h_ref = refs[3 + LEADS_PER_TILE:7 + LEADS_PER_TILE]
    rest = refs[7 + LEADS_PER_TILE:]
    if final:
        fw_ref, o_ref, pad_ref, g_ref = rest
    else:
        fw_ref = None
        o_ref, pad_ref, g_ref = rest
    del first_tbl
    tile = pl.program_id(0) + tile0
    tm = ROW_TILE
    next_row = jnp.where(zero_next_ref[tile] == 1, 0.0, cun_ref[:1, :].astype(F32))
    pad_ref[_HALO - 1:_HALO, :] = cup_ref[BF16_SUBLANES - 1:, :].astype(F32)
    pad_ref[_HALO:_HALO + tm, :] = cu_ref[...].astype(F32)
    pad_ref[_HALO + tm:_HALO + tm + 1, :] = next_row

    @pl.when(is_lead_ref[tile] == 1)
    def _():
        for k, f_ref in enumerate(first_refs):
            at = _HALO + (k + 1) * BLOCK
            pad_ref[at:at + 1, :] = f_ref[:1, :].astype(F32)

    for r0 in range(0, tm, ROW_CHUNK):
        rows = slice(r0, r0 + ROW_CHUNK)
        at = _HALO + r0
        y = (pad_ref[at - 1:at - 1 + ROW_CHUNK, :] * cw_ref[0:1, :]
             + pad_ref[at:at + ROW_CHUNK, :] * cw_ref[1:2, :]
             + pad_ref[at + 1:at + 1 + ROW_CHUNK, :] * cw_ref[2:3, :])
        g_ref[rows, :] = (t_ref[rows, :].astype(F32) * y).astype(BF16)
        h_new = h_ref[rows, :] + jnp.dot(g_ref[rows, :], w_ref[...], preferred_element_type=F32)
        o_ref[rows, :] = _finish(h_new, fw_ref)


def _conv_outproj(cu, t, cw, w, h, tables, tile0, n_tiles, fw=None):
    n_tbl = len(tables)
    in_rows = pl.BlockSpec((ROW_TILE, D_MODEL), lambda i, *tb: (i + tile0, 0))
    halo = (BF16_SUBLANES, CONV_WIDTH)

    def table_spec(k):
        return pl.BlockSpec(halo, lambda i, *tb: (tb[k][i + tile0], 0))

    def resident(shape):
        return pl.BlockSpec(shape, lambda i, *tb: (0,) * len(shape), pipeline_mode=pl.Buffered(1))

    in_specs = ([in_rows, table_spec(0), table_spec(1)]
                + [table_spec(4 + k) for k in range(LEADS_PER_TILE)]
                + [in_rows, resident((3, CONV_WIDTH)), resident((CONV_WIDTH, D_MODEL)), in_rows])
    args = [cu] * (3 + LEADS_PER_TILE) + [t, cw, w, h]
    if fw is not None:
        in_specs.append(resident((1, D_MODEL)))
        args.append(fw)
    grid_spec = pltpu.PrefetchScalarGridSpec(
        num_scalar_prefetch=n_tbl,
        grid=(n_tiles,),
        in_specs=in_specs,
        out_specs=pl.BlockSpec((ROW_TILE, D_MODEL), lambda i, *tb: (i, 0)),
        scratch_shapes=[
            pltpu.VMEM((ROW_TILE + 2 * _HALO, CONV_WIDTH), F32),
            pltpu.VMEM((ROW_TILE, CONV_WIDTH), BF16),
        ],
    )
    return pl.pallas_call(
        functools.partial(_conv_outproj_kernel, tile0=tile0, final=fw is not None),
        out_shape=jax.ShapeDtypeStruct((n_tiles * ROW_TILE, D_MODEL), F32),
        grid_spec=grid_spec,
        compiler_params=_params(1),
        name="conv_outproj",
    )(*tables, *args)


def _attn_w_in_bf16(w):
    kv0, z0 = ATTN_WIDTH, ATTN_WIDTH + 2 * KV_WIDTH
    return jnp.concatenate([w[:, :kv0], w[:, z0:], w[:, kv0:z0]], axis=1).astype(BF16)


def kernel(x_prompt, x_sample, meta_tokens, norm_w, attn_w_in, attn_w_out, attn_sink,
           conv_w_in, conv_w, conv_w_out, final_norm_w):
    d = D_MODEL
    groups = (x_prompt, x_sample)
    lay = Layout([x.shape[:2] for x in groups])
    assert len(lay.conv_tables()) == 4 + LEADS_PER_TILE

    lead = jnp.concatenate([jnp.zeros((LEAD - N_META, d), F32), meta_tokens.astype(F32)], axis=0)
    leads = jnp.concatenate([jnp.tile(lead, (lay.n_seq, 1)),
                             jnp.zeros(((lay.n_lead - lay.n_seq) * LEAD, d), F32)], axis=0)
    srcs = [x.reshape(-1, d) for x in groups] + [leads]

    blk_tables, key_biases = lay.block_tables()
    conv_tables = lay.conv_tables()

    depth = norm_w.shape[0]
    assert depth % 2 == 0
    h = None
    outs = None
    for i in range(depth):
        g = norm_w[i][None, :]
        j = i // 2
        if i % 2 == 0:
            w_in = _attn_w_in_bf16(attn_w_in[j])
            if h is None:
                proj, h = _attn_inproj(lay, srcs, g, w_in, FIRST_IN_TILE)
            else:
                proj = _attn_inproj(lay, [h], g, w_in, ROW_TILE)
            a = _attn_core(lay, proj, attn_sink[j], blk_tables, key_biases)
            h = _attn_outproj(a, attn_w_out[j].astype(BF16), h, 0, lay.tiles)
        else:
            cu, t = _conv_inproj(lay, h, g, conv_w_in[j].astype(BF16))
            w_out = conv_w_out[j].astype(BF16)
            if i < depth - 1:
                h = _conv_outproj(cu, t, conv_w[j], w_out, h, conv_tables, 0, lay.tiles)
            else:
                outs, tile0 = [], 0
                for x, rows in zip(groups, lay.group_rows):
                    nt = rows // ROW_TILE
                    y = _conv_outproj(cu, t, conv_w[j], w_out, h, conv_tables, tile0, nt,
                                      fw=final_norm_w[None, :])
                    outs.append(y.reshape(x.shape))
                    tile0 += nt
    return tuple(outs)
```

```python
import functools

import jax
import jax.numpy as jnp
import numpy as np
from jax import lax
from jax.experimental import pallas as pl
from jax.experimental.pallas import tpu as pltpu

D_MODEL = 2048
HEAD_DIM = 128
N_Q_HEADS = 16
N_KV_HEADS = 4
GROUP = N_Q_HEADS // N_KV_HEADS
ATTN_WIDTH = N_Q_HEADS * HEAD_DIM
KV_WIDTH = N_KV_HEADS * HEAD_DIM
ATTN_IN = 2 * ATTN_WIDTH + 2 * KV_WIDTH
CONV_WIDTH = D_MODEL
WINDOW = 128
BLOCK = 128
N_META = 16
LEAD = BLOCK
ROPE_THETA = 10000.0
EPS = 1e-6
NEG = -1e30
LOG2E = 1.4426950408889634

TILE = 256
PAIR = 2 * TILE
COL_CHUNK = 512
BF16_SUBLANES = 16
LEADS_PER_TILE = TILE // BLOCK
VMEM_LIMIT = 58 * 1024 * 1024
Q_SCALE = HEAD_DIM ** -0.5 * LOG2E

Z_COL0 = ATTN_WIDTH
K_COL0 = 2 * ATTN_WIDTH
V_COL0 = 2 * ATTN_WIDTH + KV_WIDTH

F32 = jnp.float32
BF16 = jnp.bfloat16


def _params(n_axes):
    return pltpu.CompilerParams(
        dimension_semantics=("arbitrary",) * n_axes, vmem_limit_bytes=VMEM_LIMIT)


def _resident(shape):
    return pl.BlockSpec(shape, lambda *_: (0,) * len(shape), pipeline_mode=pl.Buffered(1))


def _rmsnorm_bf16(x, g):
    ms = jnp.mean(x * x, axis=-1, keepdims=True)
    return ((x * lax.rsqrt(ms + EPS)) * g).astype(BF16)


def _silu(z):
    return z * jax.nn.sigmoid(z)


def _key_bias(j, n):
    nk = 3 * BLOCK + N_META
    r = np.arange(BLOCK)[:, None]
    c = np.arange(nk)[None, :]
    part, cc = c // BLOCK, c % BLOCK
    pos_q = j * BLOCK + r - (LEAD - N_META)
    kblk = j - 1 + part
    pos_k = kblk * BLOCK + cc - (LEAD - N_META)
    band = (kblk >= 0) & (kblk < n) & (pos_k >= 0) & (np.abs(pos_q - pos_k) <= WINDOW)
    meta = np.abs(pos_q - cc) > WINDOW
    visible = np.where(part == 3, meta, band)
    return np.where(visible, 0.0, NEG).astype(np.float32)


class Layout:
    def __init__(self, group_shapes):
        self.seq_rows = []
        self.group_rows = []
        for bsz, s in group_shapes:
            assert s % PAIR == 0
            self.seq_rows += [s] * bsz
            self.group_rows.append(bsz * s)
        self.n_seq = len(self.seq_rows)
        self.seq_row0 = np.concatenate([[0], np.cumsum(self.seq_rows)[:-1]]).astype(int)
        self.main_rows = int(sum(self.seq_rows))
        per_pair = PAIR // LEAD
        self.n_lead = -(-self.n_seq // per_pair) * per_pair
        self.rows = self.main_rows + self.n_lead * LEAD
        self.tiles = self.rows // TILE
        self.pairs = self.rows // PAIR
        self.main_blocks = self.main_rows // BLOCK
        self.blocks = self.rows // BLOCK

    def lead_row0(self, s):
        return self.main_rows + s * LEAD

    def block_tables(self):
        prev, nxt, lead, case = (np.zeros(self.blocks, np.int32) for _ in range(4))
        biases = []

        def bias_case(j, n):
            bias = _key_bias(j, n)
            for k, known in enumerate(biases):
                if np.array_equal(known, bias):
                    return k
            biases.append(bias)
            return len(biases) - 1

        for s in range(self.n_lead):
            lb = self.main_blocks + s
            real = s < self.n_seq
            first = self.seq_row0[s] // BLOCK if real else lb
            n = self.seq_rows[s] // BLOCK + 1 if real else 1
            prev[lb], nxt[lb], lead[lb], case[lb] = lb, first, lb, bias_case(0, n)
            for j in range(1, n):
                b = first + j - 1
                prev[b] = lb if j == 1 else b - 1
                nxt[b] = b + 1 if j < n - 1 else b
                lead[b], case[b] = lb, bias_case(j, n)
        tables = tuple(jnp.asarray(t) for t in (prev, nxt, lead, case))
        return tables, jnp.asarray(np.stack(biases))

    def conv_tables(self):
        per = TILE // BF16_SUBLANES
        main_tiles = self.main_rows // TILE
        prev = np.zeros(self.tiles, np.int32)
        nxt = np.zeros(self.tiles, np.int32)
        zero_next = np.zeros(self.tiles, np.int32)
        is_lead = np.zeros(self.tiles, np.int32)
        first = np.zeros((LEADS_PER_TILE, self.tiles), np.int32)
        starts = {int(r): s for s, r in enumerate(self.seq_row0)}
        ends = {int(r + n) for r, n in zip(self.seq_row0, self.seq_rows)}
        for i in range(self.tiles):
            row0 = i * TILE
            nxt[i] = min((i + 1) * per, self.rows // BF16_SUBLANES - 1)
            if i < main_tiles:
                if row0 in starts:
                    prev[i] = (self.lead_row0(starts[row0]) + LEAD) // BF16_SUBLANES - 1
                else:
                    prev[i] = i * per - 1
                zero_next[i] = int(row0 + TILE in ends)
            else:
                prev[i] = i * per - 1
                is_lead[i] = 1
                for k in range(LEADS_PER_TILE):
                    s = (i - main_tiles) * LEADS_PER_TILE + k
                    first[k, i] = self.seq_row0[s] // BF16_SUBLANES if s < self.n_seq else 0
        return tuple(jnp.asarray(t) for t in (prev, nxt, zero_next, is_lead, *first))

    def rope_tables(self, tm):
        longest = max(self.seq_rows)
        pos_main = np.arange(longest) + N_META
        pos_lead = np.tile(np.arange(LEAD) - (LEAD - N_META), tm // LEAD)
        pos = jnp.asarray(np.concatenate([pos_main, pos_lead]), jnp.int32)
        half = HEAD_DIM // 2
        inv_freq = ROPE_THETA ** (-jnp.arange(0, half, dtype=F32) * (2.0 / HEAD_DIM))
        ang = pos.astype(F32)[:, None] * inv_freq[None, :]
        cos = jnp.concatenate([jnp.cos(ang), jnp.cos(ang)], -1)
        sin = jnp.concatenate([-jnp.sin(ang), jnp.sin(ang)], -1)
        tile = np.zeros(self.rows // tm, np.int32)
        for s in range(self.n_seq):
            t0 = self.seq_row0[s] // tm
            for k in range(self.seq_rows[s] // tm):
                tile[t0 + k] = k
        tile[self.main_rows // tm:] = longest // tm
        return cos, sin, jnp.asarray(tile)


def _attn_project(hn_ref, w_ref, cos, sin, o_ref, rows):
    cos_q = cos * Q_SCALE
    sin_q = sin * Q_SCALE
    for c in range(ATTN_IN // COL_CHUNK):
        cols = slice(c * COL_CHUNK, (c + 1) * COL_CHUNK)
        acc = jnp.dot(hn_ref[...], w_ref[:, cols], preferred_element_type=F32)
        col0 = c * COL_CHUNK
        is_q = col0 < Z_COL0
        if is_q or K_COL0 <= col0 < V_COL0:
            cs, sn = (cos_q, sin_q) if is_q else (cos, sin)
            heads = []
            for hh in range(COL_CHUNK // HEAD_DIM):
                xh = acc[:, hh * HEAD_DIM:(hh + 1) * HEAD_DIM]
                heads.append(xh * cs + pltpu.roll(xh, HEAD_DIM // 2, axis=1) * sn)
            acc = jnp.concatenate(heads, axis=1)
        elif Z_COL0 <= col0 < K_COL0:
            acc = _silu(acc)
        o_ref[rows, cols] = acc.astype(BF16)


def _attn_inproj_first_kernel(tile_ref, *refs, src_tiles):
    n_src = len(src_tiles)
    x_refs = refs[:n_src]
    g_ref, w_ref, cos_ref, sin_ref, o_ref, h_ref, hn_ref = refs[n_src:]
    i = pl.program_id(0)
    t0 = 0
    for x_ref, nt in zip(x_refs, src_tiles):
        @pl.when((i >= t0) & (i < t0 + nt))
        def _(x_ref=x_ref):
            x = x_ref[...]
            hn_ref[...] = _rmsnorm_bf16(x, g_ref[...])
            h_ref[...] = x
        t0 += nt
    _attn_project(hn_ref, w_ref, cos_ref[...], sin_ref[...], o_ref, slice(None))


def _attn_inproj_first(lay, srcs, g, w):
    tm = TILE
    cos, sin, rope_tile = lay.rope_tables(tm)
    src_tiles = [x.shape[0] // tm for x in srcs]
    row_spec = pl.BlockSpec((tm, D_MODEL), lambda i, rt: (i, 0))
    x_specs, t0 = [], 0
    for nt in src_tiles:
        x_specs.append(pl.BlockSpec(
            (tm, D_MODEL), lambda i, rt, t0=t0, nt=nt: (jnp.clip(i - t0, 0, nt - 1), 0)))
        t0 += nt
    assert t0 == lay.tiles
    rope_spec = pl.BlockSpec((tm, HEAD_DIM), lambda i, rt: (rt[i], 0))
    grid_spec = pltpu.PrefetchScalarGridSpec(
        num_scalar_prefetch=1,
        grid=(lay.tiles,),
        in_specs=x_specs + [_resident((1, D_MODEL)), _resident((D_MODEL, ATTN_IN)),
                            rope_spec, rope_spec],
        out_specs=(pl.BlockSpec((tm, ATTN_IN), lambda i, rt: (i, 0)), row_spec),
        scratch_shapes=[pltpu.VMEM((tm, D_MODEL), BF16)],
    )
    return pl.pallas_call(
        functools.partial(_attn_inproj_first_kernel, src_tiles=tuple(src_tiles)),
        out_shape=(jax.ShapeDtypeStruct((lay.rows, ATTN_IN), BF16),
                   jax.ShapeDtypeStruct((lay.rows, D_MODEL), F32)),
        grid_spec=grid_spec,
        compiler_params=_params(1),
        name="attn_inproj_first",
    )(rope_tile, *srcs, g, w, cos, sin)


def _pair_index_maps(last_tile, pair0=0):
    def tile_b(s, *_):
        return (jnp.clip(2 * (s + pair0) - 1, 0, last_tile), 0)

    def tile_n(s, *_):
        return (jnp.minimum(2 * (s + pair0), last_tile), 0)

    def pair_in(s, *_):
        return (jnp.maximum(s - 1, 0) + pair0, 0)

    def pair_out(s, *_):
        return (jnp.maximum(s - 1, 0), 0)

    return tile_b, tile_n, pair_in, pair_out


def _attn_inproj_kernel(tile_ref, xb_ref, xn_ref, g_ref, w_ref, cos_ref, sin_ref, o_ref,
                        hna_ref, hnb_ref):
    s = pl.program_id(0)

    @pl.when(s == 0)
    def _():
        hna_ref[...] = _rmsnorm_bf16(xn_ref[...], g_ref[...])
        o_ref[...] = jnp.zeros_like(o_ref)

    @pl.when(s > 0)
    def _():
        lo, hi = slice(0, TILE), slice(TILE, PAIR)
        hnb_ref[...] = _rmsnorm_bf16(xb_ref[...], g_ref[...])
        _attn_project(hna_ref, w_ref, cos_ref[lo, :], sin_ref[lo, :], o_ref, lo)
        hna_ref[...] = _rmsnorm_bf16(xn_ref[...], g_ref[...])
        _attn_project(hnb_ref, w_ref, cos_ref[hi, :], sin_ref[hi, :], o_ref, hi)


def _attn_inproj(lay, h, g, w):
    cos, sin, rope_pair = lay.rope_tables(PAIR)
    tile_b, tile_n, _, pair_out = _pair_index_maps(lay.tiles - 1)
    x_tile = (TILE, D_MODEL)
    rope_spec = pl.BlockSpec((PAIR, HEAD_DIM), lambda s, rt: (rt[jnp.maximum(s - 1, 0)], 0))
    grid_spec = pltpu.PrefetchScalarGridSpec(
        num_scalar_prefetch=1,
        grid=(lay.pairs + 1,),
        in_specs=[pl.BlockSpec(x_tile, tile_b), pl.BlockSpec(x_tile, tile_n),
                  _resident((1, D_MODEL)), _resident((D_MODEL, ATTN_IN)), rope_spec, rope_spec],
        out_specs=pl.BlockSpec((PAIR, ATTN_IN), pair_out),
        scratch_shapes=[pltpu.VMEM(x_tile, BF16), pltpu.VMEM(x_tile, BF16)],
    )
    return pl.pallas_call(
        _attn_inproj_kernel,
        out_shape=jax.ShapeDtypeStruct((lay.rows, ATTN_IN), BF16),
        grid_spec=grid_spec,
        compiler_params=_params(1),
        name="attn_inproj",
    )(rope_pair, h, h, g, w, cos, sin)


def _attn_core_kernel(prev_ref, next_ref, lead_ref, case_ref, sink_ref, bias_ref,
                      q_ref, kp_ref, ko_ref, kn_ref, km_ref,
                      vp_ref, vo_ref, vn_ref, vm_ref, gate_ref, o_ref):
    bias = bias_ref[0]
    for h in range(N_KV_HEADS):
        kv = slice(h * HEAD_DIM, (h + 1) * HEAD_DIM)
        kcat = jnp.concatenate([kp_ref[:, kv], ko_ref[:, kv], kn_ref[:, kv], km_ref[:, kv]], axis=0)
        vcat = jnp.concatenate([vp_ref[:, kv], vo_ref[:, kv], vn_ref[:, kv], vm_ref[:, kv]], axis=0)
        heads = [slice((h * GROUP + g) * HEAD_DIM, (h * GROUP + g + 1) * HEAD_DIM)
                 for g in range(GROUP)]
        qs = jnp.concatenate([q_ref[:, cols] for cols in heads], axis=0)
        s_all = lax.dot_general(qs, kcat, (((1,), (1,)), ((), ())), preferred_element_type=F32)
        probs, inv = [], []
        for g in range(GROUP):
            s = s_all[g * BLOCK:(g + 1) * BLOCK] + bias
            sink = sink_ref[h * GROUP + g] * LOG2E
            m = jnp.maximum(jnp.max(s, axis=-1, keepdims=True), sink)
            e = jnp.exp2(s - m)
            inv.append(1.0 / (jnp.sum(e, axis=-1, keepdims=True) + jnp.exp2(sink - m)))
            probs.append(e.astype(BF16))
        o_all = jnp.dot(jnp.concatenate(probs, axis=0), vcat, preferred_element_type=F32)
        for g, cols in enumerate(heads):
            o = o_all[g * BLOCK:(g + 1) * BLOCK] * inv[g]
            o_ref[:, cols] = (o * gate_ref[:, cols].astype(F32)).astype(BF16)


def _attn_core(lay, proj, sink, tables, biases):
    z_col = Z_COL0 // ATTN_WIDTH
    k_col = K_COL0 // KV_WIDTH
    v_col = V_COL0 // KV_WIDTH
    meta_blk = (LEAD - N_META) // BF16_SUBLANES
    per_lead = BLOCK // BF16_SUBLANES

    def own(col):
        return lambda b, pv, nx, ld, cs: (b, col)

    def prev(col):
        return lambda b, pv, nx, ld, cs: (pv[b], col)

    def nxt(col):
        return lambda b, pv, nx, ld, cs: (nx[b], col)

    def meta(col):
        return lambda b, pv, nx, ld, cs: (ld[b] * per_lead + meta_blk, col)

    kv_blk = (BLOCK, KV_WIDTH)
    meta_kv = (N_META, KV_WIDTH)
    grid_spec = pltpu.PrefetchScalarGridSpec(
        num_scalar_prefetch=4,
        grid=(lay.blocks,),
        in_specs=[
            pl.BlockSpec(memory_space=pltpu.SMEM),
            pl.BlockSpec((1,) + biases.shape[1:], lambda b, pv, nx, ld, cs: (cs[b], 0, 0)),
            pl.BlockSpec((BLOCK, ATTN_WIDTH), own(0)),
            pl.BlockSpec(kv_blk, prev(k_col)),
            pl.BlockSpec(kv_blk, own(k_col)),
            pl.BlockSpec(kv_blk, nxt(k_col)),
            pl.BlockSpec(meta_kv, meta(k_col)),
            pl.BlockSpec(kv_blk, prev(v_col)),
            pl.BlockSpec(kv_blk, own(v_col)),
            pl.BlockSpec(kv_blk, nxt(v_col)),
            pl.BlockSpec(meta_kv, meta(v_col)),
            pl.BlockSpec((BLOCK, ATTN_WIDTH), own(z_col)),
        ],
        out_specs=pl.BlockSpec((BLOCK, ATTN_WIDTH), own(0)),
    )
    return pl.pallas_call(
        _attn_core_kernel,
        out_shape=jax.ShapeDtypeStruct((lay.rows, ATTN_WIDTH), BF16),
        grid_spec=grid_spec,
        compiler_params=_params(1),
        name="attn_core",
    )(*tables, sink, biases, *([proj] * 10))


def _finish(h_new, fw_ref):
    if fw_ref is None:
        return h_new
    ms = jnp.mean(h_new * h_new, axis=-1, keepdims=True)
    return (h_new * lax.rsqrt(ms + EPS)) * fw_ref[...]


def _attn_outproj_kernel(a_ref, w_ref, h_ref, o_ref):
    o_ref[...] = h_ref[...] + jnp.dot(a_ref[...], w_ref[...], preferred_element_type=F32)


def _attn_outproj(lay, a, w, h):
    rows = pl.BlockSpec((PAIR, D_MODEL), lambda i: (i, 0))
    return pl.pallas_call(
        _attn_outproj_kernel,
        out_shape=jax.ShapeDtypeStruct((lay.rows, D_MODEL), F32),
        grid=(lay.pairs,),
        in_specs=[rows, _resident((ATTN_WIDTH, D_MODEL)), rows],
        out_specs=rows,
        compiler_params=_params(1),
        name="attn_outproj",
    )(a, w, h)


def _conv_project(hn_ref, w_ref, ok, cu_ref, t_ref, rows):
    for c in range(CONV_WIDTH // COL_CHUNK):
        def part(p):
            cols = slice(p * CONV_WIDTH + c * COL_CHUNK, p * CONV_WIDTH + (c + 1) * COL_CHUNK)
            return jnp.dot(hn_ref[...], w_ref[:, cols], preferred_element_type=F32)
        cols = slice(c * COL_CHUNK, (c + 1) * COL_CHUNK)
        cu_ref[rows, cols] = jnp.where(ok, part(1) * part(2), 0.0).astype(BF16)
        t_ref[rows, cols] = (part(0) * _silu(part(3))).astype(BF16)


def _conv_inproj_kernel(xb_ref, xn_ref, g_ref, w_ref, cu_ref, t_ref, hna_ref, hnb_ref, *,
                        main_pairs):
    s = pl.program_id(0)

    @pl.when(s == 0)
    def _():
        hna_ref[...] = _rmsnorm_bf16(xn_ref[...], g_ref[...])
        cu_ref[...] = jnp.zeros_like(cu_ref)
        t_ref[...] = jnp.zeros_like(t_ref)

    @pl.when(s > 0)
    def _():
        r = lax.broadcasted_iota(jnp.int32, (TILE, 1), 0)
        first_ok = jnp.where(s - 1 < main_pairs, 0, LEAD - N_META)
        ok = r % BLOCK >= first_ok
        hnb_ref[...] = _rmsnorm_bf16(xb_ref[...], g_ref[...])
        _conv_project(hna_ref, w_ref, ok, cu_ref, t_ref, slice(0, TILE))
        hna_ref[...] = _rmsnorm_bf16(xn_ref[...], g_ref[...])
        _conv_project(hnb_ref, w_ref, ok, cu_ref, t_ref, slice(TILE, PAIR))


def _conv_inproj(lay, h, g, w):
    tile_b, tile_n, _, pair_out = _pair_index_maps(lay.tiles - 1)
    x_tile = (TILE, D_MODEL)
    out = jax.ShapeDtypeStruct((lay.rows, CONV_WIDTH), BF16)
    out_spec = pl.BlockSpec((PAIR, CONV_WIDTH), pair_out)
    return pl.pallas_call(
        functools.partial(_conv_inproj_kernel, main_pairs=lay.main_rows // PAIR),
        out_shape=(out, out),
        grid=(lay.pairs + 1,),
        in_specs=[pl.BlockSpec(x_tile, tile_b), pl.BlockSpec(x_tile, tile_n),
                  _resident((1, D_MODEL)), _resident((D_MODEL, 4 * CONV_WIDTH))],
        out_specs=(out_spec, out_spec),
        scratch_shapes=[pltpu.VMEM(x_tile, BF16), pltpu.VMEM(x_tile, BF16)],
        compiler_params=_params(1),
        name="conv_inproj",
    )(h, h, g, w)


_N_CONV_IN = 4 + LEADS_PER_TILE
F32_SUBLANES = 8


def _conv_gate(tile, zero_next_ref, is_lead_ref, refs, cw_ref, g_ref):
    cu_ref, cup_ref, cun_ref = refs[:3]
    first_refs = refs[3:3 + LEADS_PER_TILE]
    t_ref = refs[3 + LEADS_PER_TILE]
    sub = F32_SUBLANES
    r8 = lax.broadcasted_iota(jnp.int32, (sub, CONV_WIDTH), 0)
    is_lead = is_lead_ref[tile] == 1
    prev_row = cup_ref[BF16_SUBLANES - 1:, :].astype(F32)
    next_row = jnp.where(zero_next_ref[tile] == 1, 0.0, cun_ref[:1, :].astype(F32))
    x = cu_ref[...].astype(F32)

    for k, f_ref in enumerate(first_refs):
        first_tok = jnp.where(is_lead, f_ref[:1, :].astype(F32), 0.0)
        at = (k + 1) * BLOCK
        if at < TILE:
            patch_row = jnp.where(is_lead, 0, -1)
            patched = jnp.where(r8 == patch_row, first_tok, x[at:at + sub])
            x = jnp.concatenate([x[:at], patched, x[at + sub:]], axis=0)
        else:
            next_row = jnp.where(is_lead, first_tok, next_row)

    up = pltpu.roll(x, 1, axis=0)
    dn = pltpu.roll(x, TILE - 1, axis=0)
    up = jnp.concatenate([jnp.where(r8 == 0, prev_row, up[:sub]), up[sub:]], axis=0)
    dn = jnp.concatenate([dn[:TILE - sub], jnp.where(r8 == sub - 1, next_row, dn[TILE - sub:])],
                         axis=0)
    y = up * cw_ref[0:1, :] + x * cw_ref[1:2, :] + dn * cw_ref[2:3, :]
    g_ref[...] = (t_ref[...].astype(F32) * y).astype(BF16)


def _conv_outproj_kernel(*refs, pair0, last_tile, final):
    n_tbl = 4 + LEADS_PER_TILE
    zero_next_ref, is_lead_ref = refs[2], refs[3]
    refs = refs[n_tbl:]
    b_refs, n_refs = refs[:_N_CONV_IN], refs[_N_CONV_IN:2 * _N_CONV_IN]
    cw_ref, w_ref, h_ref = refs[2 * _N_CONV_IN:2 * _N_CONV_IN + 3]
    rest = refs[2 * _N_CONV_IN + 3:]
    if final:
        fw_ref, o_ref, ga_ref, gb_ref = rest
    else:
        fw_ref = None
        o_ref, ga_ref, gb_ref = rest
    s = pl.program_id(0)
    tile_b = jnp.clip(2 * (s + pair0) - 1, 0, last_tile)
    tile_n = jnp.minimum(2 * (s + pair0), last_tile)

    def project(g_ref, rows):
        h_new = h_ref[rows, :] + jnp.dot(g_ref[...], w_ref[...], preferred_element_type=F32)
        o_ref[rows, :] = _finish(h_new, fw_ref)

    @pl.when(s == 0)
    def _():
        _conv_gate(tile_n, zero_next_ref, is_lead_ref, n_refs, cw_ref, ga_ref)
        o_ref[...] = jnp.zeros_like(o_ref)

    @pl.when(s > 0)
    def _():
        _conv_gate(tile_b, zero_next_ref, is_lead_ref, b_refs, cw_ref, gb_ref)
        project(ga_ref, slice(0, TILE))
        _conv_gate(tile_n, zero_next_ref, is_lead_ref, n_refs, cw_ref, ga_ref)
        project(gb_ref, slice(TILE, PAIR))


def _conv_outproj(lay, cu, t, cw, w, h, tables, pair0, n_pairs, fw=None):
    last_tile = lay.tiles - 1
    tile_b, tile_n, pair_in, pair_out = _pair_index_maps(last_tile, pair0)
    halo = (BF16_SUBLANES, CONV_WIDTH)
    tile_blk = (TILE, CONV_WIDTH)

    def table_spec(k, tile_of):
        return pl.BlockSpec(halo, lambda s, *tb: (tb[k][tile_of(s)[0]], 0))

    def tile_specs(tile_of):
        return ([pl.BlockSpec(tile_blk, tile_of), table_spec(0, tile_of), table_spec(1, tile_of)]
                + [table_spec(4 + k, tile_of) for k in range(LEADS_PER_TILE)]
                + [pl.BlockSpec(tile_blk, tile_of)])

    def resident(shape):
        return pl.BlockSpec(shape, lambda s, *tb: (0,) * len(shape), pipeline_mode=pl.Buffered(1))

    tile_args = [cu] * (3 + LEADS_PER_TILE) + [t]
    in_specs = (tile_specs(tile_b) + tile_specs(tile_n)
                + [resident((3, CONV_WIDTH)), resident((CONV_WIDTH, D_MODEL)),
                   pl.BlockSpec((PAIR, D_MODEL), pair_in)])
    args = tile_args + tile_args + [cw, w, h]
    if fw is not None:
        in_specs.append(resident((1, D_MODEL)))
        args.append(fw)
    grid_spec = pltpu.PrefetchScalarGridSpec(
        num_scalar_prefetch=len(tables),
        grid=(n_pairs + 1,),
        in_specs=in_specs,
        out_specs=pl.BlockSpec((PAIR, D_MODEL), pair_out),
        scratch_shapes=[pltpu.VMEM(tile_blk, BF16), pltpu.VMEM(tile_blk, BF16)],
    )
    return pl.pallas_call(
        functools.partial(_conv_outproj_kernel, pair0=pair0, last_tile=last_tile,
                          final=fw is not None),
        out_shape=jax.ShapeDtypeStruct((n_pairs * PAIR, D_MODEL), F32),
        grid_spec=grid_spec,
        compiler_params=_params(1),
        name="conv_outproj",
    )(*tables, *args)


def _attn_w_in_bf16(w):
    kv0, z0 = ATTN_WIDTH, ATTN_WIDTH + 2 * KV_WIDTH
    return jnp.concatenate([w[:, :kv0], w[:, z0:], w[:, kv0:z0]], axis=1).astype(BF16)


def kernel(x_prompt, x_sample, meta_tokens, norm_w, attn_w_in, attn_w_out, attn_sink,
           conv_w_in, conv_w, conv_w_out, final_norm_w):
    d = D_MODEL
    groups = (x_prompt, x_sample)
    lay = Layout([x.shape[:2] for x in groups])

    lead = jnp.concatenate([jnp.zeros((LEAD - N_META, d), F32), meta_tokens.astype(F32)], axis=0)
    leads = jnp.concatenate([jnp.tile(lead, (lay.n_seq, 1)),
                             jnp.zeros(((lay.n_lead - lay.n_seq) * LEAD, d), F32)], axis=0)
    srcs = [x.reshape(-1, d) for x in groups] + [leads]

    blk_tables, key_biases = lay.block_tables()
    conv_tables = lay.conv_tables()
    assert len(conv_tables) == 4 + LEADS_PER_TILE

    depth = norm_w.shape[0]
    assert depth % 2 == 0
    h = None
    outs = None
    for i in range(depth):
        g = norm_w[i][None, :]
        j = i // 2
        if i % 2 == 0:
            w_in = _attn_w_in_bf16(attn_w_in[j])
            if h is None:
                proj, h = _attn_inproj_first(lay, srcs, g, w_in)
            else:
                proj = _attn_inproj(lay, h, g, w_in)
            a = _attn_core(lay, proj, attn_sink[j], blk_tables, key_biases)
            h = _attn_outproj(lay, a, attn_w_out[j].astype(BF16), h)
        else:
            cu, t = _conv_inproj(lay, h, g, conv_w_in[j].astype(BF16))
            w_out = conv_w_out[j].astype(BF16)
            if i < depth - 1:
                h = _conv_outproj(lay, cu, t, conv_w[j], w_out, h, conv_tables, 0, lay.pairs)
            else:
                outs, pair0 = [], 0
                for x, rows in zip(groups, lay.group_rows):
                    n_pairs = rows // PAIR
                    y = _conv_outproj(lay, cu, t, conv_w[j], w_out, h, conv_tables, pair0,
                                      n_pairs, fw=final_norm_w[None, :])
                    outs.append(y.reshape(x.shape))
                    pair0 += n_pairs
    return tuple(outs)
```

```python
import functools

import jax
import jax.numpy as jnp
import numpy as np
from jax import lax
from jax.experimental import pallas as pl
from jax.experimental.pallas import tpu as pltpu

D_MODEL = 2048
HEAD_DIM = 128
N_Q_HEADS = 16
N_KV_HEADS = 4
GROUP = N_Q_HEADS // N_KV_HEADS
ATTN_WIDTH = N_Q_HEADS * HEAD_DIM
KV_WIDTH = N_KV_HEADS * HEAD_DIM
ATTN_IN = 2 * ATTN_WIDTH + 2 * KV_WIDTH
CONV_WIDTH = D_MODEL
WINDOW = 128
BLOCK = 128
N_META = 16
LEAD = BLOCK
ROPE_THETA = 10000.0
EPS = 1e-6
NEG = -1e30
LOG2E = 1.4426950408889634

TILE = 256
PAIR = 2 * TILE
COL_CHUNK = 512
BF16_SUBLANES = 16
LEADS_PER_TILE = TILE // BLOCK
VMEM_LIMIT = 58 * 1024 * 1024
Q_SCALE = HEAD_DIM ** -0.5 * LOG2E

Z_COL0 = ATTN_WIDTH
K_COL0 = 2 * ATTN_WIDTH
V_COL0 = 2 * ATTN_WIDTH + KV_WIDTH

F32 = jnp.float32
BF16 = jnp.bfloat16


def _params(n_axes):
    return pltpu.CompilerParams(
        dimension_semantics=("arbitrary",) * n_axes, vmem_limit_bytes=VMEM_LIMIT)


def _resident(shape):
    return pl.BlockSpec(shape, lambda *_: (0,) * len(shape), pipeline_mode=pl.Buffered(1))


def _rmsnorm_bf16(x, g):
    ms = jnp.mean(x * x, axis=-1, keepdims=True)
    return ((x * lax.rsqrt(ms + EPS)) * g).astype(BF16)


def _silu(z):
    return z * jax.nn.sigmoid(z)


def _key_bias(j, n):
    nk = 3 * BLOCK + N_META
    r = np.arange(BLOCK)[:, None]
    c = np.arange(nk)[None, :]
    part, cc = c // BLOCK, c % BLOCK
    pos_q = j * BLOCK + r - (LEAD - N_META)
    kblk = j - 1 + part
    pos_k = kblk * BLOCK + cc - (LEAD - N_META)
    band = (kblk >= 0) & (kblk < n) & (pos_k >= 0) & (np.abs(pos_q - pos_k) <= WINDOW)
    meta = np.abs(pos_q - cc) > WINDOW
    visible = np.where(part == 3, meta, band)
    return np.where(visible, 0.0, NEG).astype(np.float32).T


class Layout:
    def __init__(self, group_shapes):
        self.seq_rows = []
        self.group_rows = []
        for bsz, s in group_shapes:
            assert s % PAIR == 0
            self.seq_rows += [s] * bsz
            self.group_rows.append(bsz * s)
        self.n_seq = len(self.seq_rows)
        self.seq_row0 = np.concatenate([[0], np.cumsum(self.seq_rows)[:-1]]).astype(int)
        self.main_rows = int(sum(self.seq_rows))
        per_pair = PAIR // LEAD
        self.n_lead = -(-self.n_seq // per_pair) * per_pair
        self.rows = self.main_rows + self.n_lead * LEAD
        self.tiles = self.rows // TILE
        self.pairs = self.rows // PAIR
        self.main_blocks = self.main_rows // BLOCK
        self.blocks = self.rows // BLOCK

    def lead_row0(self, s):
        return self.main_rows + s * LEAD

    def block_tables(self):
        prev, nxt, lead, case = (np.zeros(self.blocks, np.int32) for _ in range(4))
        biases = []

        def bias_case(j, n):
            bias = _key_bias(j, n)
            for k, known in enumerate(biases):
                if np.array_equal(known, bias):
                    return k
            biases.append(bias)
            return len(biases) - 1

        for s in range(self.n_lead):
            lb = self.main_blocks + s
            real = s < self.n_seq
            first = self.seq_row0[s] // BLOCK if real else lb
            n = self.seq_rows[s] // BLOCK + 1 if real else 1
            prev[lb], nxt[lb], lead[lb], case[lb] = lb, first, lb, bias_case(0, n)
            for j in range(1, n):
                b = first + j - 1
                prev[b] = lb if j == 1 else b - 1
                nxt[b] = b + 1 if j < n - 1 else b
                lead[b], case[b] = lb, bias_case(j, n)
        tables = tuple(jnp.asarray(t) for t in (prev, nxt, lead, case))
        return tables, jnp.asarray(np.stack(biases))

    def conv_tables(self):
        per = TILE // BF16_SUBLANES
        main_tiles = self.main_rows // TILE
        prev = np.zeros(self.tiles, np.int32)
        nxt = np.zeros(self.tiles, np.int32)
        zero_next = np.zeros(self.tiles, np.int32)
        is_lead = np.zeros(self.tiles, np.int32)
        first = np.zeros((LEADS_PER_TILE, self.tiles), np.int32)
        starts = {int(r): s for s, r in enumerate(self.seq_row0)}
        ends = {int(r + n) for r, n in zip(self.seq_row0, self.seq_rows)}
        for i in range(self.tiles):
            row0 = i * TILE
            nxt[i] = min((i + 1) * per, self.rows // BF16_SUBLANES - 1)
            if i < main_tiles:
                if row0 in starts:
                    prev[i] = (self.lead_row0(starts[row0]) + LEAD) // BF16_SUBLANES - 1
                else:
                    prev[i] = i * per - 1
                zero_next[i] = int(row0 + TILE in ends)
            else:
                prev[i] = i * per - 1
                is_lead[i] = 1
                for k in range(LEADS_PER_TILE):
                    s = (i - main_tiles) * LEADS_PER_TILE + k
                    first[k, i] = self.seq_row0[s] // BF16_SUBLANES if s < self.n_seq else 0
        return tuple(jnp.asarray(t) for t in (prev, nxt, zero_next, is_lead, *first))

    def rope_tables(self, tm):
        longest = max(self.seq_rows)
        pos_main = np.arange(longest) + N_META
        pos_lead = np.tile(np.arange(LEAD) - (LEAD - N_META), tm // LEAD)
        pos = jnp.asarray(np.concatenate([pos_main, pos_lead]), jnp.int32)
        half = HEAD_DIM // 2
        inv_freq = ROPE_THETA ** (-jnp.arange(0, half, dtype=F32) * (2.0 / HEAD_DIM))
        ang = pos.astype(F32)[:, None] * inv_freq[None, :]
        cos = jnp.concatenate([jnp.cos(ang), jnp.cos(ang)], -1)
        sin = jnp.concatenate([-jnp.sin(ang), jnp.sin(ang)], -1)
        tile = np.zeros(self.rows // tm, np.int32)
        for s in range(self.n_seq):
            t0 = self.seq_row0[s] // tm
            for k in range(self.seq_rows[s] // tm):
                tile[t0 + k] = k
        tile[self.main_rows // tm:] = longest // tm
        return cos, sin, jnp.asarray(tile)


def _attn_project(hn_ref, w_ref, cos, sin, o_ref, rows):
    cos_q = cos * Q_SCALE
    sin_q = sin * Q_SCALE
    for c in range(ATTN_IN // COL_CHUNK):
        cols = slice(c * COL_CHUNK, (c + 1) * COL_CHUNK)
        acc = jnp.dot(hn_ref[...], w_ref[:, cols], preferred_element_type=F32)
        col0 = c * COL_CHUNK
        is_q = col0 < Z_COL0
        if is_q or K_COL0 <= col0 < V_COL0:
            cs, sn = (cos_q, sin_q) if is_q else (cos, sin)
            heads = []
            for hh in range(COL_CHUNK // HEAD_DIM):
                xh = acc[:, hh * HEAD_DIM:(hh + 1) * HEAD_DIM]
                heads.append(xh * cs + pltpu.roll(xh, HEAD_DIM // 2, axis=1) * sn)
            acc = jnp.concatenate(heads, axis=1)
        elif Z_COL0 <= col0 < K_COL0:
            acc = _silu(acc)
        o_ref[rows, cols] = acc.astype(BF16)


def _attn_inproj_first_kernel(tile_ref, *refs, src_tiles):
    n_src = len(src_tiles)
    x_refs = refs[:n_src]
    g_ref, w_ref, cos_ref, sin_ref, o_ref, h_ref, hn_ref = refs[n_src:]
    i = pl.program_id(0)
    t0 = 0
    for x_ref, nt in zip(x_refs, src_tiles):
        @pl.when((i >= t0) & (i < t0 + nt))
        def _(x_ref=x_ref):
            x = x_ref[...]
            hn_ref[...] = _rmsnorm_bf16(x, g_ref[...])
            h_ref[...] = x
        t0 += nt
    _attn_project(hn_ref, w_ref, cos_ref[...], sin_ref[...], o_ref, slice(None))


def _attn_inproj_first(lay, srcs, g, w):
    tm = TILE
    cos, sin, rope_tile = lay.rope_tables(tm)
    src_tiles = [x.shape[0] // tm for x in srcs]
    row_spec = pl.BlockSpec((tm, D_MODEL), lambda i, rt: (i, 0))
    x_specs, t0 = [], 0
    for nt in src_tiles:
        x_specs.append(pl.BlockSpec(
            (tm, D_MODEL), lambda i, rt, t0=t0, nt=nt: (jnp.clip(i - t0, 0, nt - 1), 0)))
        t0 += nt
    assert t0 == lay.tiles
    rope_spec = pl.BlockSpec((tm, HEAD_DIM), lambda i, rt: (rt[i], 0))
    grid_spec = pltpu.PrefetchScalarGridSpec(
        num_scalar_prefetch=1,
        grid=(lay.tiles,),
        in_specs=x_specs + [_resident((1, D_MODEL)), _resident((D_MODEL, ATTN_IN)),
                            rope_spec, rope_spec],
        out_specs=(pl.BlockSpec((tm, ATTN_IN), lambda i, rt: (i, 0)), row_spec),
        scratch_shapes=[pltpu.VMEM((tm, D_MODEL), BF16)],
    )
    return pl.pallas_call(
        functools.partial(_attn_inproj_first_kernel, src_tiles=tuple(src_tiles)),
        out_shape=(jax.ShapeDtypeStruct((lay.rows, ATTN_IN), BF16),
                   jax.ShapeDtypeStruct((lay.rows, D_MODEL), F32)),
        grid_spec=grid_spec,
        compiler_params=_params(1),
        name="attn_inproj_first",
    )(rope_tile, *srcs, g, w, cos, sin)


def _pair_index_maps(last_tile, pair0=0):
    def tile_b(s, *_):
        return (jnp.clip(2 * (s + pair0) - 1, 0, last_tile), 0)

    def tile_n(s, *_):
        return (jnp.minimum(2 * (s + pair0), last_tile), 0)

    def pair_in(s, *_):
        return (jnp.maximum(s - 1, 0) + pair0, 0)

    def pair_out(s, *_):
        return (jnp.maximum(s - 1, 0), 0)

    return tile_b, tile_n, pair_in, pair_out


def _attn_inproj_kernel(tile_ref, xb_ref, xn_ref, g_ref, w_ref, cos_ref, sin_ref, o_ref,
                        hna_ref, hnb_ref):
    s = pl.program_id(0)

    @pl.when(s == 0)
    def _():
        hna_ref[...] = _rmsnorm_bf16(xn_ref[...], g_ref[...])
        o_ref[...] = jnp.zeros_like(o_ref)

    @pl.when(s > 0)
    def _():
        lo, hi = slice(0, TILE), slice(TILE, PAIR)
        hnb_ref[...] = _rmsnorm_bf16(xb_ref[...], g_ref[...])
        _attn_project(hna_ref, w_ref, cos_ref[lo, :], sin_ref[lo, :], o_ref, lo)
        hna_ref[...] = _rmsnorm_bf16(xn_ref[...], g_ref[...])
        _attn_project(hnb_ref, w_ref, cos_ref[hi, :], sin_ref[hi, :], o_ref, hi)


def _attn_inproj(lay, h, g, w):
    cos, sin, rope_pair = lay.rope_tables(PAIR)
    tile_b, tile_n, _, pair_out = _pair_index_maps(lay.tiles - 1)
    x_tile = (TILE, D_MODEL)
    rope_spec = pl.BlockSpec((PAIR, HEAD_DIM), lambda s, rt: (rt[jnp.maximum(s - 1, 0)], 0))
    grid_spec = pltpu.PrefetchScalarGridSpec(
        num_scalar_prefetch=1,
        grid=(lay.pairs + 1,),
        in_specs=[pl.BlockSpec(x_tile, tile_b), pl.BlockSpec(x_tile, tile_n),
                  _resident((1, D_MODEL)), _resident((D_MODEL, ATTN_IN)), rope_spec, rope_spec],
        out_specs=pl.BlockSpec((PAIR, ATTN_IN), pair_out),
        scratch_shapes=[pltpu.VMEM(x_tile, BF16), pltpu.VMEM(x_tile, BF16)],
    )
    return pl.pallas_call(
        _attn_inproj_kernel,
        out_shape=jax.ShapeDtypeStruct((lay.rows, ATTN_IN), BF16),
        grid_spec=grid_spec,
        compiler_params=_params(1),
        name="attn_inproj",
    )(rope_pair, h, h, g, w, cos, sin)


_REDUCE_SLABS = 5


def _reduce_rows(x, op):
    slabs = x.reshape(_REDUCE_SLABS, x.shape[0] // _REDUCE_SLABS, x.shape[1])
    return op(op(slabs, axis=0), axis=0, keepdims=True)


def _attn_core_kernel(prev_ref, next_ref, lead_ref, case_ref, sink_ref, bias_ref,
                      q_ref, kp_ref, ko_ref, kn_ref, km_ref,
                      vp_ref, vo_ref, vn_ref, vm_ref, gate_ref, o_ref):
    bias = bias_ref[0]

    def heads_of(h):
        return [slice((h * GROUP + g) * HEAD_DIM, (h * GROUP + g + 1) * HEAD_DIM)
                for g in range(GROUP)]

    def cat(refs, h):
        kv = slice(h * HEAD_DIM, (h + 1) * HEAD_DIM)
        return jnp.concatenate([r[:, kv] for r in refs], axis=0)

    scores = []
    for h in range(N_KV_HEADS):
        qs = jnp.concatenate([q_ref[:, cols] for cols in heads_of(h)], axis=0)
        scores.append(lax.dot_general(cat((kp_ref, ko_ref, kn_ref, km_ref), h), qs,
                                      (((1,), (1,)), ((), ())), preferred_element_type=F32))
    outs = []
    for h in range(N_KV_HEADS):
        probs, inv = [], []
        for g in range(GROUP):
            s = scores[h][:, g * BLOCK:(g + 1) * BLOCK] + bias
            sink = sink_ref[h * GROUP + g] * LOG2E
            m = jnp.maximum(_reduce_rows(s, jnp.max), sink)
            e = jnp.exp2(s - m)
            inv.append(1.0 / (_reduce_rows(e, jnp.sum) + jnp.exp2(sink - m)))
            probs.append(e.astype(BF16))
        o_all = lax.dot_general(cat((vp_ref, vo_ref, vn_ref, vm_ref), h),
                                jnp.concatenate(probs, axis=1), (((0,), (0,)), ((), ())),
                                preferred_element_type=F32)
        outs.append(o_all * jnp.concatenate(inv, axis=1))
    for h in range(N_KV_HEADS):
        for g, cols in enumerate(heads_of(h)):
            o = outs[h][:, g * BLOCK:(g + 1) * BLOCK].T
            o_ref[:, cols] = (o * gate_ref[:, cols].astype(F32)).astype(BF16)


def _attn_core(lay, proj, sink, tables, biases):
    z_col = Z_COL0 // ATTN_WIDTH
    k_col = K_COL0 // KV_WIDTH
    v_col = V_COL0 // KV_WIDTH
    meta_blk = (LEAD - N_META) // BF16_SUBLANES
    per_lead = BLOCK // BF16_SUBLANES

    def own(col):
        return lambda b, pv, nx, ld, cs: (b, col)

    def prev(col):
        return lambda b, pv, nx, ld, cs: (pv[b], col)

    def nxt(col):
        return lambda b, pv, nx, ld, cs: (nx[b], col)

    def meta(col):
        return lambda b, pv, nx, ld, cs: (ld[b] * per_lead + meta_blk, col)

    kv_blk = (BLOCK, KV_WIDTH)
    meta_kv = (N_META, KV_WIDTH)
    grid_spec = pltpu.PrefetchScalarGridSpec(
        num_scalar_prefetch=4,
        grid=(lay.blocks,),
        in_specs=[
            pl.BlockSpec(memory_space=pltpu.SMEM),
            pl.BlockSpec((1,) + biases.shape[1:], lambda b, pv, nx, ld, cs: (cs[b], 0, 0)),
            pl.BlockSpec((BLOCK, ATTN_WIDTH), own(0)),
            pl.BlockSpec(kv_blk, prev(k_col)),
            pl.BlockSpec(kv_blk, own(k_col)),
            pl.BlockSpec(kv_blk, nxt(k_col)),
            pl.BlockSpec(meta_kv, meta(k_col)),
            pl.BlockSpec(kv_blk, prev(v_col)),
            pl.BlockSpec(kv_blk, own(v_col)),
            pl.BlockSpec(kv_blk, nxt(v_col)),
            pl.BlockSpec(meta_kv, meta(v_col)),
            pl.BlockSpec((BLOCK, ATTN_WIDTH), own(z_col)),
        ],
        out_specs=pl.BlockSpec((BLOCK, ATTN_WIDTH), own(0)),
    )
    return pl.pallas_call(
        _attn_core_kernel,
        out_shape=jax.ShapeDtypeStruct((lay.rows, ATTN_WIDTH), BF16),
        grid_spec=grid_spec,
        compiler_params=_params(1),
        name="attn_core",
    )(*tables, sink, biases, *([proj] * 10))


def _finish(h_new, fw_ref):
    if fw_ref is None:
        return h_new
    ms = jnp.mean(h_new * h_new, axis=-1, keepdims=True)
    return (h_new * lax.rsqrt(ms + EPS)) * fw_ref[...]


def _attn_outproj_kernel(a_ref, w_ref, h_ref, o_ref):
    o_ref[...] = h_ref[...] + jnp.dot(a_ref[...], w_ref[...], preferred_element_type=F32)


def _attn_outproj(lay, a, w, h):
    rows = pl.BlockSpec((PAIR, D_MODEL), lambda i: (i, 0))
    return pl.pallas_call(
        _attn_outproj_kernel,
        out_shape=jax.ShapeDtypeStruct((lay.rows, D_MODEL), F32),
        grid=(lay.pairs,),
        in_specs=[rows, _resident((ATTN_WIDTH, D_MODEL)), rows],
        out_specs=rows,
        compiler_params=_params(1),
        name="attn_outproj",
    )(a, w, h)


def _conv_project(hn_ref, w_ref, ok, cu_ref, t_ref, rows):
    for c in range(CONV_WIDTH // COL_CHUNK):
        def part(p):
            cols = slice(p * CONV_WIDTH + c * COL_CHUNK, p * CONV_WIDTH + (c + 1) * COL_CHUNK)
            return jnp.dot(hn_ref[...], w_ref[:, cols], preferred_element_type=F32)
        cols = slice(c * COL_CHUNK, (c + 1) * COL_CHUNK)
        cu_ref[rows, cols] = jnp.where(ok, part(1) * part(2), 0.0).astype(BF16)
        t_ref[rows, cols] = (part(0) * _silu(part(3))).astype(BF16)


def _conv_inproj_kernel(xb_ref, xn_ref, g_ref, w_ref, cu_ref, t_ref, hna_ref, hnb_ref, *,
                        main_pairs):
    s = pl.program_id(0)

    @pl.when(s == 0)
    def _():
        hna_ref[...] = _rmsnorm_bf16(xn_ref[...], g_ref[...])
        cu_ref[...] = jnp.zeros_like(cu_ref)
        t_ref[...] = jnp.zeros_like(t_ref)

    @pl.when(s > 0)
    def _():
        r = lax.broadcasted_iota(jnp.int32, (TILE, 1), 0)
        first_ok = jnp.where(s - 1 < main_pairs, 0, LEAD - N_META)
        ok = r % BLOCK >= first_ok
        hnb_ref[...] = _rmsnorm_bf16(xb_ref[...], g_ref[...])
        _conv_project(hna_ref, w_ref, ok, cu_ref, t_ref, slice(0, TILE))
        hna_ref[...] = _rmsnorm_bf16(xn_ref[...], g_ref[...])
        _conv_project(hnb_ref, w_ref, ok, cu_ref, t_ref, slice(TILE, PAIR))


def _conv_inproj(lay, h, g, w):
    tile_b, tile_n, _, pair_out = _pair_index_maps(lay.tiles - 1)
    x_tile = (TILE, D_MODEL)
    out = jax.ShapeDtypeStruct((lay.rows, CONV_WIDTH), BF16)
    out_spec = pl.BlockSpec((PAIR, CONV_WIDTH), pair_out)
    return pl.pallas_call(
        functools.partial(_conv_inproj_kernel, main_pairs=lay.main_rows // PAIR),
        out_shape=(out, out),
        grid=(lay.pairs + 1,),
        in_specs=[pl.BlockSpec(x_tile, tile_b), pl.BlockSpec(x_tile, tile_n),
                  _resident((1, D_MODEL)), _resident((D_MODEL, 4 * CONV_WIDTH))],
        out_specs=(out_spec, out_spec),
        scratch_shapes=[pltpu.VMEM(x_tile, BF16), pltpu.VMEM(x_tile, BF16)],
        compiler_params=_params(1),
        name="conv_inproj",
    )(h, h, g, w)


_N_CONV_IN = 4 + LEADS_PER_TILE
F32_SUBLANES = 8


def _conv_gate(tile, zero_next_ref, is_lead_ref, refs, cw_ref, g_ref):
    cu_ref, cup_ref, cun_ref = refs[:3]
    first_refs = refs[3:3 + LEADS_PER_TILE]
    t_ref = refs[3 + LEADS_PER_TILE]
    sub = F32_SUBLANES
    r8 = lax.broadcasted_iota(jnp.int32, (sub, CONV_WIDTH), 0)
    is_lead = is_lead_ref[tile] == 1
    prev_row = cup_ref[BF16_SUBLANES - 1:, :].astype(F32)
    next_row = jnp.where(zero_next_ref[tile] == 1, 0.0, cun_ref[:1, :].astype(F32))
    x = cu_ref[...].astype(F32)

    for k, f_ref in enumerate(first_refs):
        first_tok = jnp.where(is_lead, f_ref[:1, :].astype(F32), 0.0)
        at = (k + 1) * BLOCK
        if at < TILE:
            patch_row = jnp.where(is_lead, 0, -1)
            patched = jnp.where(r8 == patch_row, first_tok, x[at:at + sub])
            x = jnp.concatenate([x[:at], patched, x[at + sub:]], axis=0)
        else:
            next_row = jnp.where(is_lead, first_tok, next_row)

    up = pltpu.roll(x, 1, axis=0)
    dn = pltpu.roll(x, TILE - 1, axis=0)
    up = jnp.concatenate([jnp.where(r8 == 0, prev_row, up[:sub]), up[sub:]], axis=0)
    dn = jnp.concatenate([dn[:TILE - sub], jnp.where(r8 == sub - 1, next_row, dn[TILE - sub:])],
                         axis=0)
    y = up * cw_ref[0:1, :] + x * cw_ref[1:2, :] + dn * cw_ref[2:3, :]
    g_ref[...] = (t_ref[...].astype(F32) * y).astype(BF16)


def _conv_outproj_kernel(*refs, pair0, last_tile, final):
    n_tbl = 4 + LEADS_PER_TILE
    zero_next_ref, is_lead_ref = refs[2], refs[3]
    refs = refs[n_tbl:]
    b_refs, n_refs = refs[:_N_CONV_IN], refs[_N_CONV_IN:2 * _N_CONV_IN]
    cw_ref, w_ref, h_ref = refs[2 * _N_CONV_IN:2 * _N_CONV_IN + 3]
    rest = refs[2 * _N_CONV_IN + 3:]
    if final:
        fw_ref, o_ref, ga_ref, gb_ref = rest
    else:
        fw_ref = None
        o_ref, ga_ref, gb_ref = rest
    s = pl.program_id(0)
    tile_b = jnp.clip(2 * (s + pair0) - 1, 0, last_tile)
    tile_n = jnp.minimum(2 * (s + pair0), last_tile)

    def project(g_ref, rows):
        h_new = h_ref[rows, :] + jnp.dot(g_ref[...], w_ref[...], preferred_element_type=F32)
        o_ref[rows, :] = _finish(h_new, fw_ref)

    @pl.when(s == 0)
    def _():
        _conv_gate(tile_n, zero_next_ref, is_lead_ref, n_refs, cw_ref, ga_ref)
        o_ref[...] = jnp.zeros_like(o_ref)

    @pl.when(s > 0)
    def _():
        _conv_gate(tile_b, zero_next_ref, is_lead_ref, b_refs, cw_ref, gb_ref)
        project(ga_ref, slice(0, TILE))
        _conv_gate(tile_n, zero_next_ref, is_lead_ref, n_refs, cw_ref, ga_ref)
        project(gb_ref, slice(TILE, PAIR))


def _conv_outproj(lay, cu, t, cw, w, h, tables, pair0, n_pairs, fw=None):
    last_tile = lay.tiles - 1
    tile_b, tile_n, pair_in, pair_out = _pair_index_maps(last_tile, pair0)
    halo = (BF16_SUBLANES, CONV_WIDTH)
    tile_blk = (TILE, CONV_WIDTH)

    def table_spec(k, tile_of):
        return pl.BlockSpec(halo, lambda s, *tb: (tb[k][tile_of(s)[0]], 0))

    def tile_specs(tile_of):
        return ([pl.BlockSpec(tile_blk, tile_of), table_spec(0, tile_of), table_spec(1, tile_of)]
                + [table_spec(4 + k, tile_of) for k in range(LEADS_PER_TILE)]
                + [pl.BlockSpec(tile_blk, tile_of)])

    def resident(shape):
        return pl.BlockSpec(shape, lambda s, *tb: (0,) * len(shape), pipeline_mode=pl.Buffered(1))

    tile_args = [cu] * (3 + LEADS_PER_TILE) + [t]
    in_specs = (tile_specs(tile_b) + tile_specs(tile_n)
                + [resident((3, CONV_WIDTH)), resident((CONV_WIDTH, D_MODEL)),
                   pl.BlockSpec((PAIR, D_MODEL), pair_in)])
    args = tile_args + tile_args + [cw, w, h]
    if fw is not None:
        in_specs.append(resident((1, D_MODEL)))
        args.append(fw)
    grid_spec = pltpu.PrefetchScalarGridSpec(
        num_scalar_prefetch=len(tables),
        grid=(n_pairs + 1,),
        in_specs=in_specs,
        out_specs=pl.BlockSpec((PAIR, D_MODEL), pair_out),
        scratch_shapes=[pltpu.VMEM(tile_blk, BF16), pltpu.VMEM(tile_blk, BF16)],
    )
    return pl.pallas_call(
        functools.partial(_conv_outproj_kernel, pair0=pair0, last_tile=last_tile,
                          final=fw is not None),
        out_shape=jax.ShapeDtypeStruct((n_pairs * PAIR, D_MODEL), F32),
        grid_spec=grid_spec,
        compiler_params=_params(1),
        name="conv_outproj",
    )(*tables, *args)


def _attn_w_in_bf16(w):
    kv0, z0 = ATTN_WIDTH, ATTN_WIDTH + 2 * KV_WIDTH
    return jnp.concatenate([w[:, :kv0], w[:, z0:], w[:, kv0:z0]], axis=1).astype(BF16)


def kernel(x_prompt, x_sample, meta_tokens, norm_w, attn_w_in, attn_w_out, attn_sink,
           conv_w_in, conv_w, conv_w_out, final_norm_w):
    d = D_MODEL
    groups = (x_prompt, x_sample)
    lay = Layout([x.shape[:2] for x in groups])

    lead = jnp.concatenate([jnp.zeros((LEAD - N_META, d), F32), meta_tokens.astype(F32)], axis=0)
    leads = jnp.concatenate([jnp.tile(lead, (lay.n_seq, 1)),
                             jnp.zeros(((lay.n_lead - lay.n_seq) * LEAD, d), F32)], axis=0)
    srcs = [x.reshape(-1, d) for x in groups] + [leads]

    blk_tables, key_biases = lay.block_tables()
    conv_tables = lay.conv_tables()
    assert len(conv_tables) == 4 + LEADS_PER_TILE

    depth = norm_w.shape[0]
    assert depth % 2 == 0
    h = None
    outs = None
    for i in range(depth):
        g = norm_w[i][None, :]
        j = i // 2
        if i % 2 == 0:
            w_in = _attn_w_in_bf16(attn_w_in[j])
            if h is None:
                proj, h = _attn_inproj_first(lay, srcs, g, w_in)
            else:
                proj = _attn_inproj(lay, h, g, w_in)
            a = _attn_core(lay, proj, attn_sink[j], blk_tables, key_biases)
            h = _attn_outproj(lay, a, attn_w_out[j].astype(BF16), h)
        else:
            cu, t = _conv_inproj(lay, h, g, conv_w_in[j].astype(BF16))
            w_out = conv_w_out[j].astype(BF16)
            if i < depth - 1:
                h = _conv_outproj(lay, cu, t, conv_w[j], w_out, h, conv_tables, 0, lay.pairs)
            else:
                outs, pair0 = [], 0
                for x, rows in zip(groups, lay.group_rows):
                    n_pairs = rows // PAIR
                    y = _conv_outproj(lay, cu, t, conv_w[j], w_out, h, conv_tables, pair0,
                                      n_pairs, fw=final_norm_w[None, :])
                    outs.append(y.reshape(x.shape))
                    pair0 += n_pairs
    return tuple(outs)
```

```python
import functools

import jax
import jax.numpy as jnp
import numpy as np
from jax import lax
from jax.experimental import pallas as pl
from jax.experimental.pallas import tpu as pltpu

D_MODEL = 2048
HEAD_DIM = 128
N_Q_HEADS = 16
N_KV_HEADS = 4
GROUP = N_Q_HEADS // N_KV_HEADS
ATTN_WIDTH = N_Q_HEADS * HEAD_DIM
KV_WIDTH = N_KV_HEADS * HEAD_DIM
ATTN_IN = 2 * ATTN_WIDTH + 2 * KV_WIDTH
CONV_WIDTH = D_MODEL
WINDOW = 128
BLOCK = 128
N_META = 16
LEAD = BLOCK
ROPE_THETA = 10000.0
EPS = 1e-6
NEG = -1e30
LOG2E = 1.4426950408889634

TILE = 256
PAIR = 2 * TILE
COL_CHUNK = 512
BF16_SUBLANES = 16
LEADS_PER_TILE = TILE // BLOCK
VMEM_LIMIT = 58 * 1024 * 1024
Q_SCALE = HEAD_DIM ** -0.5 * LOG2E

Z_COL0 = ATTN_WIDTH
K_COL0 = 2 * ATTN_WIDTH
V_COL0 = 2 * ATTN_WIDTH + KV_WIDTH

F32 = jnp.float32
BF16 = jnp.bfloat16


def _params(n_axes):
    return pltpu.CompilerParams(
        dimension_semantics=("arbitrary",) * n_axes, vmem_limit_bytes=VMEM_LIMIT)


def _resident(shape):
    return pl.BlockSpec(shape, lambda *_: (0,) * len(shape), pipeline_mode=pl.Buffered(1))


def _rmsnorm_bf16(x, g):
    ms = jnp.mean(x * x, axis=-1, keepdims=True)
    return ((x * lax.rsqrt(ms + EPS)) * g).astype(BF16)


def _silu(z):
    return z * jax.nn.sigmoid(z)


def _key_bias(j, n):
    nk = 3 * BLOCK + N_META
    r = np.arange(BLOCK)[:, None]
    c = np.arange(nk)[None, :]
    part, cc = c // BLOCK, c % BLOCK
    pos_q = j * BLOCK + r - (LEAD - N_META)
    kblk = j - 1 + part
    pos_k = kblk * BLOCK + cc - (LEAD - N_META)
    band = (kblk >= 0) & (kblk < n) & (pos_k >= 0) & (np.abs(pos_q - pos_k) <= WINDOW)
    meta = np.abs(pos_q - cc) > WINDOW
    visible = np.where(part == 3, meta, band)
    return np.where(visible, 0.0, NEG).astype(np.float32).T


class Layout:
    def __init__(self, group_shapes):
        self.seq_rows = []
        self.group_rows = []
        for bsz, s in group_shapes:
            assert s % PAIR == 0
            self.seq_rows += [s] * bsz
            self.group_rows.append(bsz * s)
        self.n_seq = len(self.seq_rows)
        self.seq_row0 = np.concatenate([[0], np.cumsum(self.seq_rows)[:-1]]).astype(int)
        self.main_rows = int(sum(self.seq_rows))
        per_pair = PAIR // LEAD
        self.n_lead = -(-self.n_seq // per_pair) * per_pair
        self.rows = self.main_rows + self.n_lead * LEAD
        self.tiles = self.rows // TILE
        self.pairs = self.rows // PAIR
        self.main_blocks = self.main_rows // BLOCK
        self.blocks = self.rows // BLOCK

    def lead_row0(self, s):
        return self.main_rows + s * LEAD

    def block_tables(self):
        prev, nxt, lead, case = (np.zeros(self.blocks, np.int32) for _ in range(4))
        biases = []

        def bias_case(j, n):
            bias = _key_bias(j, n)
            for k, known in enumerate(biases):
                if np.array_equal(known, bias):
                    return k
            biases.append(bias)
            return len(biases) - 1

        for s in range(self.n_lead):
            lb = self.main_blocks + s
            real = s < self.n_seq
            first = self.seq_row0[s] // BLOCK if real else lb
            n = self.seq_rows[s] // BLOCK + 1 if real else 1
            prev[lb], nxt[lb], lead[lb], case[lb] = lb, first, lb, bias_case(0, n)
            for j in range(1, n):
                b = first + j - 1
                prev[b] = lb if j == 1 else b - 1
                nxt[b] = b + 1 if j < n - 1 else b
                lead[b], case[b] = lb, bias_case(j, n)
        tables = tuple(jnp.asarray(t) for t in (prev, nxt, lead, case))
        return tables, jnp.asarray(np.stack(biases))

    def conv_tables(self):
        per = TILE // BF16_SUBLANES
        main_tiles = self.main_rows // TILE
        prev = np.zeros(self.tiles, np.int32)
        nxt = np.zeros(self.tiles, np.int32)
        zero_next = np.zeros(self.tiles, np.int32)
        is_lead = np.zeros(self.tiles, np.int32)
        first = np.zeros((LEADS_PER_TILE, self.tiles), np.int32)
        starts = {int(r): s for s, r in enumerate(self.seq_row0)}
        ends = {int(r + n) for r, n in zip(self.seq_row0, self.seq_rows)}
        for i in range(self.tiles):
            row0 = i * TILE
            nxt[i] = min((i + 1) * per, self.rows // BF16_SUBLANES - 1)
            if i < main_tiles:
                if row0 in starts:
                    prev[i] = (self.lead_row0(starts[row0]) + LEAD) // BF16_SUBLANES - 1
                else:
                    prev[i] = i * per - 1
                zero_next[i] = int(row0 + TILE in ends)
            else:
                prev[i] = i * per - 1
                is_lead[i] = 1
                for k in range(LEADS_PER_TILE):
                    s = (i - main_tiles) * LEADS_PER_TILE + k
                    first[k, i] = self.seq_row0[s] // BF16_SUBLANES if s < self.n_seq else 0
        return tuple(jnp.asarray(t) for t in (prev, nxt, zero_next, is_lead, *first))

    def rope_tables(self, tm):
        longest = max(self.seq_rows)
        pos_main = np.arange(longest) + N_META
        pos_lead = np.tile(np.arange(LEAD) - (LEAD - N_META), tm // LEAD)
        pos = jnp.asarray(np.concatenate([pos_main, pos_lead]), jnp.int32)
        half = HEAD_DIM // 2
        inv_freq = ROPE_THETA ** (-jnp.arange(0, half, dtype=F32) * (2.0 / HEAD_DIM))
        ang = pos.astype(F32)[:, None] * inv_freq[None, :]
        cos = jnp.concatenate([jnp.cos(ang), jnp.cos(ang)], -1)
        sin = jnp.concatenate([-jnp.sin(ang), jnp.sin(ang)], -1)
        tile = np.zeros(self.rows // tm, np.int32)
        for s in range(self.n_seq):
            t0 = self.seq_row0[s] // tm
            for k in range(self.seq_rows[s] // tm):
                tile[t0 + k] = k
        tile[self.main_rows // tm:] = longest // tm
        return cos, sin, jnp.asarray(tile)


def _attn_project(hn_ref, w_ref, cos, sin, o_ref, rows):
    cos_q = cos * Q_SCALE
    sin_q = sin * Q_SCALE
    for c in range(ATTN_IN // COL_CHUNK):
        cols = slice(c * COL_CHUNK, (c + 1) * COL_CHUNK)
        acc = jnp.dot(hn_ref[...], w_ref[:, cols], preferred_element_type=F32)
        col0 = c * COL_CHUNK
        is_q = col0 < Z_COL0
        if is_q or K_COL0 <= col0 < V_COL0:
            cs, sn = (cos_q, sin_q) if is_q else (cos, sin)
            heads = []
            for hh in range(COL_CHUNK // HEAD_DIM):
                xh = acc[:, hh * HEAD_DIM:(hh + 1) * HEAD_DIM]
                heads.append(xh * cs + pltpu.roll(xh, HEAD_DIM // 2, axis=1) * sn)
            acc = jnp.concatenate(heads, axis=1)
        elif Z_COL0 <= col0 < K_COL0:
            acc = _silu(acc)
        o_ref[rows, cols] = acc.astype(BF16)


def _attn_inproj_first_kernel(tile_ref, *refs, src_tiles):
    n_src = len(src_tiles)
    x_refs = refs[:n_src]
    g_ref, w_ref, cos_ref, sin_ref, o_ref, h_ref, hn_ref = refs[n_src:]
    i = pl.program_id(0)
    t0 = 0
    for x_ref, nt in zip(x_refs, src_tiles):
        @pl.when((i >= t0) & (i < t0 + nt))
        def _(x_ref=x_ref):
            x = x_ref[...]
            hn_ref[...] = _rmsnorm_bf16(x, g_ref[...])
            h_ref[...] = x
        t0 += nt
    _attn_project(hn_ref, w_ref, cos_ref[...], sin_ref[...], o_ref, slice(None))


def _attn_inproj_first(lay, srcs, g, w):
    tm = TILE
    cos, sin, rope_tile = lay.rope_tables(tm)
    src_tiles = [x.shape[0] // tm for x in srcs]
    row_spec = pl.BlockSpec((tm, D_MODEL), lambda i, rt: (i, 0))
    x_specs, t0 = [], 0
    for nt in src_tiles:
        x_specs.append(pl.BlockSpec(
            (tm, D_MODEL), lambda i, rt, t0=t0, nt=nt: (jnp.clip(i - t0, 0, nt - 1), 0)))
        t0 += nt
    assert t0 == lay.tiles
    rope_spec = pl.BlockSpec((tm, HEAD_DIM), lambda i, rt: (rt[i], 0))
    grid_spec = pltpu.PrefetchScalarGridSpec(
        num_scalar_prefetch=1,
        grid=(lay.tiles,),
        in_specs=x_specs + [_resident((1, D_MODEL)), _resident((D_MODEL, ATTN_IN)),
                            rope_spec, rope_spec],
        out_specs=(pl.BlockSpec((tm, ATTN_IN), lambda i, rt: (i, 0)), row_spec),
        scratch_shapes=[pltpu.VMEM((tm, D_MODEL), BF16)],
    )
    return pl.pallas_call(
        functools.partial(_attn_inproj_first_kernel, src_tiles=tuple(src_tiles)),
        out_shape=(jax.ShapeDtypeStruct((lay.rows, ATTN_IN), BF16),
                   jax.ShapeDtypeStruct((lay.rows, D_MODEL), F32)),
        grid_spec=grid_spec,
        compiler_params=_params(1),
        name="attn_inproj_first",
    )(rope_tile, *srcs, g, w, cos, sin)


def _pair_index_maps(last_tile, pair0=0):
    def tile_b(s, *_):
        return (jnp.clip(2 * (s + pair0) - 1, 0, last_tile), 0)

    def tile_n(s, *_):
        return (jnp.minimum(2 * (s + pair0), last_tile), 0)

    def pair_in(s, *_):
        return (jnp.maximum(s - 1, 0) + pair0, 0)

    def pair_out(s, *_):
        return (jnp.maximum(s - 1, 0), 0)

    return tile_b, tile_n, pair_in, pair_out


def _attn_inproj_kernel(tile_ref, xb_ref, xn_ref, g_ref, w_ref, cos_ref, sin_ref, o_ref,
                        hna_ref, hnb_ref):
    s = pl.program_id(0)

    @pl.when(s == 0)
    def _():
        hna_ref[...] = _rmsnorm_bf16(xn_ref[...], g_ref[...])
        o_ref[...] = jnp.zeros_like(o_ref)

    @pl.when(s > 0)
    def _():
        lo, hi = slice(0, TILE), slice(TILE, PAIR)
        hnb_ref[...] = _rmsnorm_bf16(xb_ref[...], g_ref[...])
        _attn_project(hna_ref, w_ref, cos_ref[lo, :], sin_ref[lo, :], o_ref, lo)
        hna_ref[...] = _rmsnorm_bf16(xn_ref[...], g_ref[...])
        _attn_project(hnb_ref, w_ref, cos_ref[hi, :], sin_ref[hi, :], o_ref, hi)


def _attn_inproj(lay, h, g, w):
    cos, sin, rope_pair = lay.rope_tables(PAIR)
    tile_b, tile_n, _, pair_out = _pair_index_maps(lay.tiles - 1)
    x_tile = (TILE, D_MODEL)
    rope_spec = pl.BlockSpec((PAIR, HEAD_DIM), lambda s, rt: (rt[jnp.maximum(s - 1, 0)], 0))
    grid_spec = pltpu.PrefetchScalarGridSpec(
        num_scalar_prefetch=1,
        grid=(lay.pairs + 1,),
        in_specs=[pl.BlockSpec(x_tile, tile_b), pl.BlockSpec(x_tile, tile_n),
                  _resident((1, D_MODEL)), _resident((D_MODEL, ATTN_IN)), rope_spec, rope_spec],
        out_specs=pl.BlockSpec((PAIR, ATTN_IN), pair_out),
        scratch_shapes=[pltpu.VMEM(x_tile, BF16), pltpu.VMEM(x_tile, BF16)],
    )
    return pl.pallas_call(
        _attn_inproj_kernel,
        out_shape=jax.ShapeDtypeStruct((lay.rows, ATTN_IN), BF16),
        grid_spec=grid_spec,
        compiler_params=_params(1),
        name="attn_inproj",
    )(rope_pair, h, h, g, w, cos, sin)


_REDUCE_SLABS = 5


def _reduce_rows(x, op):
    slabs = x.reshape(_REDUCE_SLABS, x.shape[0] // _REDUCE_SLABS, x.shape[1])
    return op(op(slabs, axis=0), axis=0, keepdims=True)


_BLOCKS_PER_STEP = 2


def _attn_core_kernel(prev_ref, next_ref, lead_ref, case_ref, sink_ref, *refs):
    n_in = 2 + 4 * _BLOCKS_PER_STEP
    qz_ref, kvo_ref = refs[:2]
    bias_refs = refs[2:2 + _BLOCKS_PER_STEP]
    nbr_refs = refs[2 + _BLOCKS_PER_STEP:n_in]
    o_ref = refs[n_in]

    def heads_of(h):
        return [slice((h * GROUP + g) * HEAD_DIM, (h * GROUP + g + 1) * HEAD_DIM)
                for g in range(GROUP)]

    def keys_of(blk, col0, h):
        kvp_ref, kvn_ref, kvm_ref = nbr_refs[3 * blk:3 * blk + 3]
        cols = slice(col0 + h * HEAD_DIM, col0 + (h + 1) * HEAD_DIM)
        own = kvo_ref[blk * BLOCK:(blk + 1) * BLOCK, cols]
        return jnp.concatenate([kvp_ref[:, cols], own, kvn_ref[:, cols], kvm_ref[:, cols]], axis=0)

    scores = {}
    outs = {}
    for blk in range(_BLOCKS_PER_STEP):
        rows = slice(blk * BLOCK, (blk + 1) * BLOCK)
        for h in range(N_KV_HEADS):
            qs = jnp.concatenate([qz_ref[rows, cols] for cols in heads_of(h)], axis=0)
            scores[blk, h] = lax.dot_general(keys_of(blk, 0, h), qs, (((1,), (1,)), ((), ())),
                                             preferred_element_type=F32)
        bias = bias_refs[blk][0]
        for h in range(N_KV_HEADS):
            probs, inv = [], []
            for g in range(GROUP):
                s = scores[blk, h][:, g * BLOCK:(g + 1) * BLOCK] + bias
                sink = sink_ref[h * GROUP + g] * LOG2E
                m = jnp.maximum(_reduce_rows(s, jnp.max), sink)
                e = jnp.exp2(s - m)
                inv.append(1.0 / (_reduce_rows(e, jnp.sum) + jnp.exp2(sink - m)))
                probs.append(e.astype(BF16))
            o_all = lax.dot_general(keys_of(blk, KV_WIDTH, h), jnp.concatenate(probs, axis=1),
                                    (((0,), (0,)), ((), ())), preferred_element_type=F32)
            outs[blk, h] = o_all * jnp.concatenate(inv, axis=1)
    for blk in range(_BLOCKS_PER_STEP):
        rows = slice(blk * BLOCK, (blk + 1) * BLOCK)
        for h in range(N_KV_HEADS):
            for g, cols in enumerate(heads_of(h)):
                o = outs[blk, h][:, g * BLOCK:(g + 1) * BLOCK].T
                gate = qz_ref[rows, Z_COL0 + cols.start:Z_COL0 + cols.stop].astype(F32)
                o_ref[rows, cols] = (o * gate).astype(BF16)


def _attn_core(lay, proj, sink, tables, biases):
    n = _BLOCKS_PER_STEP
    assert lay.blocks % n == 0 and K_COL0 % (2 * KV_WIDTH) == 0 and V_COL0 == K_COL0 + KV_WIDTH
    kv_col = K_COL0 // (2 * KV_WIDTH)
    meta_blk = (LEAD - N_META) // BF16_SUBLANES
    per_lead = BLOCK // BF16_SUBLANES

    def step_rows(col):
        return lambda i, pv, nx, ld, cs: (i, col)

    def bias_of(k):
        return lambda i, pv, nx, ld, cs: (cs[n * i + k], 0, 0)

    def prev(k):
        return lambda i, pv, nx, ld, cs: (pv[n * i + k], kv_col)

    def nxt(k):
        return lambda i, pv, nx, ld, cs: (nx[n * i + k], kv_col)

    def meta(k):
        return lambda i, pv, nx, ld, cs: (ld[n * i + k] * per_lead + meta_blk, kv_col)

    kv_blk = (BLOCK, 2 * KV_WIDTH)
    nbr_specs = []
    for k in range(n):
        nbr_specs += [pl.BlockSpec(kv_blk, prev(k)), pl.BlockSpec(kv_blk, nxt(k)),
                      pl.BlockSpec((N_META, 2 * KV_WIDTH), meta(k))]
    grid_spec = pltpu.PrefetchScalarGridSpec(
        num_scalar_prefetch=4,
        grid=(lay.blocks // n,),
        in_specs=[pl.BlockSpec(memory_space=pltpu.SMEM),
                  pl.BlockSpec((n * BLOCK, K_COL0), step_rows(0)),
                  pl.BlockSpec((n * BLOCK, 2 * KV_WIDTH), step_rows(kv_col))]
        + [pl.BlockSpec((1,) + biases.shape[1:], bias_of(k)) for k in range(n)] + nbr_specs,
        out_specs=pl.BlockSpec((n * BLOCK, ATTN_WIDTH), step_rows(0)),
    )
    return pl.pallas_call(
        _attn_core_kernel,
        out_shape=jax.ShapeDtypeStruct((lay.rows, ATTN_WIDTH), BF16),
        grid_spec=grid_spec,
        compiler_params=_params(1),
        name="attn_core",
    )(*tables, sink, proj, proj, *([biases] * n), *([proj] * (3 * n)))


def _finish(h_new, fw_ref):
    if fw_ref is None:
        return h_new
    ms = jnp.mean(h_new * h_new, axis=-1, keepdims=True)
    return (h_new * lax.rsqrt(ms + EPS)) * fw_ref[...]


def _attn_outproj_kernel(a_ref, w_ref, h_ref, o_ref):
    o_ref[...] = h_ref[...] + jnp.dot(a_ref[...], w_ref[...], preferred_element_type=F32)


def _attn_outproj(lay, a, w, h):
    rows = pl.BlockSpec((PAIR, D_MODEL), lambda i: (i, 0))
    return pl.pallas_call(
        _attn_outproj_kernel,
        out_shape=jax.ShapeDtypeStruct((lay.rows, D_MODEL), F32),
        grid=(lay.pairs,),
        in_specs=[rows, _resident((ATTN_WIDTH, D_MODEL)), rows],
        out_specs=rows,
        compiler_params=_params(1),
        name="attn_outproj",
    )(a, w, h)


def _conv_project(hn_ref, w_ref, ok, cu_ref, t_ref, rows):
    for c in range(CONV_WIDTH // COL_CHUNK):
        def part(p):
            cols = slice(p * CONV_WIDTH + c * COL_CHUNK, p * CONV_WIDTH + (c + 1) * COL_CHUNK)
            return jnp.dot(hn_ref[...], w_ref[:, cols], preferred_element_type=F32)
        cols = slice(c * COL_CHUNK, (c + 1) * COL_CHUNK)
        cu_ref[rows, cols] = jnp.where(ok, part(1) * part(2), 0.0).astype(BF16)
        t_ref[rows, cols] = (part(0) * _silu(part(3))).astype(BF16)


def _conv_inproj_kernel(xb_ref, xn_ref, g_ref, w_ref, cu_ref, t_ref, hna_ref, hnb_ref, *,
                        main_pairs):
    s = pl.program_id(0)

    @pl.when(s == 0)
    def _():
        hna_ref[...] = _rmsnorm_bf16(xn_ref[...], g_ref[...])
        cu_ref[...] = jnp.zeros_like(cu_ref)
        t_ref[...] = jnp.zeros_like(t_ref)

    @pl.when(s > 0)
    def _():
        r = lax.broadcasted_iota(jnp.int32, (TILE, 1), 0)
        first_ok = jnp.where(s - 1 < main_pairs, 0, LEAD - N_META)
        ok = r % BLOCK >= first_ok
        hnb_ref[...] = _rmsnorm_bf16(xb_ref[...], g_ref[...])
        _conv_project(hna_ref, w_ref, ok, cu_ref, t_ref, slice(0, TILE))
        hna_ref[...] = _rmsnorm_bf16(xn_ref[...], g_ref[...])
        _conv_project(hnb_ref, w_ref, ok, cu_ref, t_ref, slice(TILE, PAIR))


def _conv_inproj(lay, h, g, w):
    tile_b, tile_n, _, pair_out = _pair_index_maps(lay.tiles - 1)
    x_tile = (TILE, D_MODEL)
    out = jax.ShapeDtypeStruct((lay.rows, CONV_WIDTH), BF16)
    out_spec = pl.BlockSpec((PAIR, CONV_WIDTH), pair_out)
    return pl.pallas_call(
        functools.partial(_conv_inproj_kernel, main_pairs=lay.main_rows // PAIR),
        out_shape=(out, out),
        grid=(lay.pairs + 1,),
        in_specs=[pl.BlockSpec(x_tile, tile_b), pl.BlockSpec(x_tile, tile_n),
                  _resident((1, D_MODEL)), _resident((D_MODEL, 4 * CONV_WIDTH))],
        out_specs=(out_spec, out_spec),
        scratch_shapes=[pltpu.VMEM(x_tile, BF16), pltpu.VMEM(x_tile, BF16)],
        compiler_params=_params(1),
        name="conv_inproj",
    )(h, h, g, w)


_N_CONV_IN = 4 + LEADS_PER_TILE
F32_SUBLANES = 8


def _conv_gate(tile, zero_next_ref, is_lead_ref, refs, cw_ref, g_ref):
    cu_ref, cup_ref, cun_ref = refs[:3]
    first_refs = refs[3:3 + LEADS_PER_TILE]
    t_ref = refs[3 + LEADS_PER_TILE]
    sub = F32_SUBLANES
    r8 = lax.broadcasted_iota(jnp.int32, (sub, CONV_WIDTH), 0)
    is_lead = is_lead_ref[tile] == 1
    prev_row = cup_ref[BF16_SUBLANES - 1:, :].astype(F32)
    next_row = jnp.where(zero_next_ref[tile] == 1, 0.0, cun_ref[:1, :].astype(F32))
    x = cu_ref[...].astype(F32)

    for k, f_ref in enumerate(first_refs):
        first_tok = jnp.where(is_lead, f_ref[:1, :].astype(F32), 0.0)
        at = (k + 1) * BLOCK
        if at < TILE:
            patch_row = jnp.where(is_lead, 0, -1)
            patched = jnp.where(r8 == patch_row, first_tok, x[at:at + sub])
            x = jnp.concatenate([x[:at], patched, x[at + sub:]], axis=0)
        else:
            next_row = jnp.where(is_lead, first_tok, next_row)

    up = pltpu.roll(x, 1, axis=0)
    dn = pltpu.roll(x, TILE - 1, axis=0)
    up = jnp.concatenate([jnp.where(r8 == 0, prev_row, up[:sub]), up[sub:]], axis=0)
    dn = jnp.concatenate([dn[:TILE - sub], jnp.where(r8 == sub - 1, next_row, dn[TILE - sub:])],
                         axis=0)
    y = up * cw_ref[0:1, :] + x * cw_ref[1:2, :] + dn * cw_ref[2:3, :]
    g_ref[...] = (t_ref[...].astype(F32) * y).astype(BF16)


def _conv_outproj_kernel(*refs, pair0, last_tile, final):
    n_tbl = 4 + LEADS_PER_TILE
    zero_next_ref, is_lead_ref = refs[2], refs[3]
    refs = refs[n_tbl:]
    b_refs, n_refs = refs[:_N_CONV_IN], refs[_N_CONV_IN:2 * _N_CONV_IN]
    cw_ref, w_ref, h_ref = refs[2 * _N_CONV_IN:2 * _N_CONV_IN + 3]
    rest = refs[2 * _N_CONV_IN + 3:]
    if final:
        fw_ref, o_ref, ga_ref, gb_ref = rest
    else:
        fw_ref = None
        o_ref, ga_ref, gb_ref = rest
    s = pl.program_id(0)
    tile_b = jnp.clip(2 * (s + pair0) - 1, 0, last_tile)
    tile_n = jnp.minimum(2 * (s + pair0), last_tile)

    def project(g_ref, rows):
        h_new = h_ref[rows, :] + jnp.dot(g_ref[...], w_ref[...], preferred_element_type=F32)
        o_ref[rows, :] = _finish(h_new, fw_ref)

    @pl.when(s == 0)
    def _():
        _conv_gate(tile_n, zero_next_ref, is_lead_ref, n_refs, cw_ref, ga_ref)
        o_ref[...] = jnp.zeros_like(o_ref)

    @pl.when(s > 0)
    def _():
        _conv_gate(tile_b, zero_next_ref, is_lead_ref, b_refs, cw_ref, gb_ref)
        project(ga_ref, slice(0, TILE))
        _conv_gate(tile_n, zero_next_ref, is_lead_ref, n_refs, cw_ref, ga_ref)
        project(gb_ref, slice(TILE, PAIR))


def _conv_outproj(lay, cu, t, cw, w, h, tables, pair0, n_pairs, fw=None):
    last_tile = lay.tiles - 1
    tile_b, tile_n, pair_in, pair_out = _pair_index_maps(last_tile, pair0)
    halo = (BF16_SUBLANES, CONV_WIDTH)
    tile_blk = (TILE, CONV_WIDTH)

    def table_spec(k, tile_of):
        return pl.BlockSpec(halo, lambda s, *tb: (tb[k][tile_of(s)[0]], 0))

    def tile_specs(tile_of):
        return ([pl.BlockSpec(tile_blk, tile_of), table_spec(0, tile_of), table_spec(1, tile_of)]
                + [table_spec(4 + k, tile_of) for k in range(LEADS_PER_TILE)]
                + [pl.BlockSpec(tile_blk, tile_of)])

    def resident(shape):
        return pl.BlockSpec(shape, lambda s, *tb: (0,) * len(shape), pipeline_mode=pl.Buffered(1))

    tile_args = [cu] * (3 + LEADS_PER_TILE) + [t]
    in_specs = (tile_specs(tile_b) + tile_specs(tile_n)
                + [resident((3, CONV_WIDTH)), resident((CONV_WIDTH, D_MODEL)),
                   pl.BlockSpec((PAIR, D_MODEL), pair_in)])
    args = tile_args + tile_args + [cw, w, h]
    if fw is not None:
        in_specs.append(resident((1, D_MODEL)))
        args.append(fw)
    grid_spec = pltpu.PrefetchScalarGridSpec(
        num_scalar_prefetch=len(tables),
        grid=(n_pairs + 1,),
        in_specs=in_specs,
        out_specs=pl.BlockSpec((PAIR, D_MODEL), pair_out),
        scratch_shapes=[pltpu.VMEM(tile_blk, BF16), pltpu.VMEM(tile_blk, BF16)],
    )
    return pl.pallas_call(
        functools.partial(_conv_outproj_kernel, pair0=pair0, last_tile=last_tile,
                          final=fw is not None),
        out_shape=jax.ShapeDtypeStruct((n_pairs * PAIR, D_MODEL), F32),
        grid_spec=grid_spec,
        compiler_params=_params(1),
        name="conv_outproj",
    )(*tables, *args)


def _attn_w_in_bf16(w):
    kv0, z0 = ATTN_WIDTH, ATTN_WIDTH + 2 * KV_WIDTH
    return jnp.concatenate([w[:, :kv0], w[:, z0:], w[:, kv0:z0]], axis=1).astype(BF16)


def kernel(x_prompt, x_sample, meta_tokens, norm_w, attn_w_in, attn_w_out, attn_sink,
           conv_w_in, conv_w, conv_w_out, final_norm_w):
    d = D_MODEL
    groups = (x_prompt, x_sample)
    lay = Layout([x.shape[:2] for x in groups])

    lead = jnp.concatenate([jnp.zeros((LEAD - N_META, d), F32), meta_tokens.astype(F32)], axis=0)
    leads = jnp.concatenate([jnp.tile(lead, (lay.n_seq, 1)),
                             jnp.zeros(((lay.n_lead - lay.n_seq) * LEAD, d), F32)], axis=0)
    srcs = [x.reshape(-1, d) for x in groups] + [leads]

    blk_tables, key_biases = lay.block_tables()
    conv_tables = lay.conv_tables()
    assert len(conv_tables) == 4 + LEADS_PER_TILE

    depth = norm_w.shape[0]
    assert depth % 2 == 0
    h = None
    outs = None
    for i in range(depth):
        g = norm_w[i][None, :]
        j = i // 2
        if i % 2 == 0:
            w_in = _attn_w_in_bf16(attn_w_in[j])
            if h is None:
                proj, h = _attn_inproj_first(lay, srcs, g, w_in)
            else:
                proj = _attn_inproj(lay, h, g, w_in)
            a = _attn_core(lay, proj, attn_sink[j], blk_tables, key_biases)
            h = _attn_outproj(lay, a, attn_w_out[j].astype(BF16), h)
        else:
            cu, t = _conv_inproj(lay, h, g, conv_w_in[j].astype(BF16))
            w_out = conv_w_out[j].astype(BF16)
            if i < depth - 1:
                h = _conv_outproj(lay, cu, t, conv_w[j], w_out, h, conv_tables, 0, lay.pairs)
            else:
                outs, pair0 = [], 0
                for x, rows in zip(groups, lay.group_rows):
                    n_pairs = rows // PAIR
                    y = _conv_outproj(lay, cu, t, conv_w[j], w_out, h, conv_tables, pair0,
                                      n_pairs, fw=final_norm_w[None, :])
                    outs.append(y.reshape(x.shape))
                    pair0 += n_pairs
    return tuple(outs)
```

```python
import functools

import jax
import jax.numpy as jnp
import numpy as np
from jax import lax
from jax.experimental import pallas as pl
from jax.experimental.pallas import tpu as pltpu

D_MODEL = 2048
HEAD_DIM = 128
N_Q_HEADS = 16
N_KV_HEADS = 4
GROUP = N_Q_HEADS // N_KV_HEADS
ATTN_WIDTH = N_Q_HEADS * HEAD_DIM
KV_WIDTH = N_KV_HEADS * HEAD_DIM
ATTN_IN = 2 * ATTN_WIDTH + 2 * KV_WIDTH
CONV_WIDTH = D_MODEL
WINDOW = 128
BLOCK = 128
N_META = 16
LEAD = BLOCK
ROPE_THETA = 10000.0
EPS = 1e-6
NEG = -1e30
LOG2E = 1.4426950408889634

TILE = 256
PAIR = 2 * TILE
COL_CHUNK = 512
BF16_SUBLANES = 16
LEADS_PER_TILE = TILE // BLOCK
VMEM_LIMIT = 58 * 1024 * 1024
Q_SCALE = HEAD_DIM ** -0.5 * LOG2E

Z_COL0 = ATTN_WIDTH
K_COL0 = 2 * ATTN_WIDTH
V_COL0 = 2 * ATTN_WIDTH + KV_WIDTH

F32 = jnp.float32
BF16 = jnp.bfloat16


def _params(n_axes):
    return pltpu.CompilerParams(
        dimension_semantics=("arbitrary",) * n_axes, vmem_limit_bytes=VMEM_LIMIT)


def _resident(shape):
    return pl.BlockSpec(shape, lambda *_: (0,) * len(shape), pipeline_mode=pl.Buffered(1))


def _rmsnorm_bf16(x, g):
    ms = jnp.mean(x * x, axis=-1, keepdims=True)
    return ((x * lax.rsqrt(ms + EPS)) * g).astype(BF16)


def _silu(z):
    return z * jax.nn.sigmoid(z)


def _key_bias(j, n):
    nk = 3 * BLOCK + N_META
    r = np.arange(BLOCK)[:, None]
    c = np.arange(nk)[None, :]
    part, cc = c // BLOCK, c % BLOCK
    pos_q = j * BLOCK + r - (LEAD - N_META)
    kblk = j - 1 + part
    pos_k = kblk * BLOCK + cc - (LEAD - N_META)
    band = (kblk >= 0) & (kblk < n) & (pos_k >= 0) & (np.abs(pos_q - pos_k) <= WINDOW)
    meta = np.abs(pos_q - cc) > WINDOW
    visible = np.where(part == 3, meta, band)
    return np.where(visible, 0.0, NEG).astype(np.float32).T


class Layout:
    def __init__(self, group_shapes):
        self.seq_rows = []
        self.group_rows = []
        for bsz, s in group_shapes:
            assert s % PAIR == 0
            self.seq_rows += [s] * bsz
            self.group_rows.append(bsz * s)
        self.n_seq = len(self.seq_rows)
        self.seq_row0 = np.concatenate([[0], np.cumsum(self.seq_rows)[:-1]]).astype(int)
        self.main_rows = int(sum(self.seq_rows))
        per_pair = PAIR // LEAD
        self.n_lead = -(-self.n_seq // per_pair) * per_pair
        self.rows = self.main_rows + self.n_lead * LEAD
        self.tiles = self.rows // TILE
        self.pairs = self.rows // PAIR
        self.main_blocks = self.main_rows // BLOCK
        self.blocks = self.rows // BLOCK

    def lead_row0(self, s):
        return self.main_rows + s * LEAD

    def block_tables(self):
        prev, nxt, lead, case = (np.zeros(self.blocks, np.int32) for _ in range(4))
        biases = []

        def bias_case(j, n):
            bias = _key_bias(j, n)
            for k, known in enumerate(biases):
                if np.array_equal(known, bias):
                    return k
            biases.append(bias)
            return len(biases) - 1

        for s in range(self.n_lead):
            lb = self.main_blocks + s
            real = s < self.n_seq
            first = self.seq_row0[s] // BLOCK if real else lb
            n = self.seq_rows[s] // BLOCK + 1 if real else 1
            prev[lb], nxt[lb], lead[lb], case[lb] = lb, first, lb, bias_case(0, n)
            for j in range(1, n):
                b = first + j - 1
                prev[b] = lb if j == 1 else b - 1
                nxt[b] = b + 1 if j < n - 1 else b
                lead[b], case[b] = lb, bias_case(j, n)
        tables = tuple(jnp.asarray(t) for t in (prev, nxt, lead, case))
        return tables, jnp.asarray(np.stack(biases))

    def conv_tables(self):
        per = TILE // BF16_SUBLANES
        main_tiles = self.main_rows // TILE
        prev = np.zeros(self.tiles, np.int32)
        nxt = np.zeros(self.tiles, np.int32)
        zero_next = np.zeros(self.tiles, np.int32)
        is_lead = np.zeros(self.tiles, np.int32)
        first = np.zeros((LEADS_PER_TILE, self.tiles), np.int32)
        starts = {int(r): s for s, r in enumerate(self.seq_row0)}
        ends = {int(r + n) for r, n in zip(self.seq_row0, self.seq_rows)}
        for i in range(self.tiles):
            row0 = i * TILE
            nxt[i] = min((i + 1) * per, self.rows // BF16_SUBLANES - 1)
            if i < main_tiles:
                if row0 in starts:
                    prev[i] = (self.lead_row0(starts[row0]) + LEAD) // BF16_SUBLANES - 1
                else:
                    prev[i] = i * per - 1
                zero_next[i] = int(row0 + TILE in ends)
            else:
                prev[i] = i * per - 1
                is_lead[i] = 1
                for k in range(LEADS_PER_TILE):
                    s = (i - main_tiles) * LEADS_PER_TILE + k
                    first[k, i] = self.seq_row0[s] // BF16_SUBLANES if s < self.n_seq else 0
        return tuple(jnp.asarray(t) for t in (prev, nxt, zero_next, is_lead, *first))

    def rope_tables(self, tm):
        longest = max(self.seq_rows)
        pos_main = np.arange(longest) + N_META
        pos_lead = np.tile(np.arange(LEAD) - (LEAD - N_META), PAIR // LEAD)
        pos = np.concatenate([pos_main, pos_lead]).astype(np.float64)
        half = HEAD_DIM // 2
        inv_freq = ROPE_THETA ** (-np.arange(0, half, dtype=np.float64) * (2.0 / HEAD_DIM))
        ang = pos[:, None] * inv_freq[None, :]
        cos = np.concatenate([np.cos(ang), np.cos(ang)], -1).astype(np.float32)
        sin = np.concatenate([-np.sin(ang), np.sin(ang)], -1).astype(np.float32)
        tile = np.zeros(self.rows // tm, np.int32)
        for s in range(self.n_seq):
            t0 = self.seq_row0[s] // tm
            for k in range(self.seq_rows[s] // tm):
                tile[t0 + k] = k
        lead_tiles = np.arange(self.rows // tm - self.main_rows // tm)
        tile[self.main_rows // tm:] = longest // tm + lead_tiles % (PAIR // tm)
        return jnp.asarray(cos), jnp.asarray(sin), jnp.asarray(tile)


def _attn_project(hn_ref, w_ref, cos, sin, o_ref, rows):
    cos_q = cos * Q_SCALE
    sin_q = sin * Q_SCALE
    for c in range(ATTN_IN // COL_CHUNK):
        cols = slice(c * COL_CHUNK, (c + 1) * COL_CHUNK)
        acc = jnp.dot(hn_ref[...], w_ref[:, cols], preferred_element_type=F32)
        col0 = c * COL_CHUNK
        is_q = col0 < Z_COL0
        if is_q or K_COL0 <= col0 < V_COL0:
            cs, sn = (cos_q, sin_q) if is_q else (cos, sin)
            heads = []
            for hh in range(COL_CHUNK // HEAD_DIM):
                xh = acc[:, hh * HEAD_DIM:(hh + 1) * HEAD_DIM]
                heads.append(xh * cs + pltpu.roll(xh, HEAD_DIM // 2, axis=1) * sn)
            acc = jnp.concatenate(heads, axis=1)
        elif Z_COL0 <= col0 < K_COL0:
            acc = _silu(acc)
        o_ref[rows, cols] = acc.astype(BF16)


def _attn_inproj_first_kernel(tile_ref, *refs, src_tiles):
    n_src = len(src_tiles)
    x_refs = refs[:n_src]
    g_ref, w_ref, cos_ref, sin_ref, o_ref, h_ref, hn_ref = refs[n_src:]
    i = pl.program_id(0)
    t0 = 0
    for x_ref, nt in zip(x_refs, src_tiles):
        @pl.when((i >= t0) & (i < t0 + nt))
        def _(x_ref=x_ref):
            x = x_ref[...]
            hn_ref[...] = _rmsnorm_bf16(x, g_ref[...])
            h_ref[...] = x
        t0 += nt
    _attn_project(hn_ref, w_ref, cos_ref[...], sin_ref[...], o_ref, slice(None))


def _attn_inproj_first(lay, srcs, g, w):
    tm = TILE
    cos, sin, rope_tile = lay.rope_tables(tm)
    assert srcs[-1].shape[0] == tm
    src_tiles = [x.shape[0] // tm for x in srcs[:-1]]
    src_tiles.append(lay.tiles - sum(src_tiles))
    row_spec = pl.BlockSpec((tm, D_MODEL), lambda i, rt: (i, 0))
    x_specs, t0 = [], 0
    for x, nt in zip(srcs, src_tiles):
        last = x.shape[0] // tm - 1
        x_specs.append(pl.BlockSpec(
            (tm, D_MODEL), lambda i, rt, t0=t0, last=last: (jnp.clip(i - t0, 0, last), 0)))
        t0 += nt
    rope_spec = pl.BlockSpec((tm, HEAD_DIM), lambda i, rt: (rt[i], 0))
    grid_spec = pltpu.PrefetchScalarGridSpec(
        num_scalar_prefetch=1,
        grid=(lay.tiles,),
        in_specs=x_specs + [_resident((1, D_MODEL)), _resident((D_MODEL, ATTN_IN)),
                            rope_spec, rope_spec],
        out_specs=(pl.BlockSpec((tm, ATTN_IN), lambda i, rt: (i, 0)), row_spec),
        scratch_shapes=[pltpu.VMEM((tm, D_MODEL), BF16)],
    )
    return pl.pallas_call(
        functools.partial(_attn_inproj_first_kernel, src_tiles=tuple(src_tiles)),
        out_shape=(jax.ShapeDtypeStruct((lay.rows, ATTN_IN), BF16),
                   jax.ShapeDtypeStruct((lay.rows, D_MODEL), F32)),
        grid_spec=grid_spec,
        compiler_params=_params(1),
        name="attn_inproj_first",
    )(rope_tile, *srcs, g, w, cos, sin)


def _pair_index_maps(last_tile, pair0=0):
    def tile_b(s, *_):
        return (jnp.clip(2 * (s + pair0) - 1, 0, last_tile), 0)

    def tile_n(s, *_):
        return (jnp.minimum(2 * (s + pair0), last_tile), 0)

    def pair_in(s, *_):
        return (jnp.maximum(s - 1, 0) + pair0, 0)

    def pair_out(s, *_):
        return (jnp.maximum(s - 1, 0), 0)

    return tile_b, tile_n, pair_in, pair_out


def _attn_inproj_kernel(tile_ref, xb_ref, xn_ref, g_ref, w_ref, cos_ref, sin_ref, o_ref,
                        hna_ref, hnb_ref):
    s = pl.program_id(0)

    @pl.when(s == 0)
    def _():
        hna_ref[...] = _rmsnorm_bf16(xn_ref[...], g_ref[...])
        o_ref[...] = jnp.zeros_like(o_ref)

    @pl.when(s > 0)
    def _():
        lo, hi = slice(0, TILE), slice(TILE, PAIR)
        hnb_ref[...] = _rmsnorm_bf16(xb_ref[...], g_ref[...])
        _attn_project(hna_ref, w_ref, cos_ref[lo, :], sin_ref[lo, :], o_ref, lo)
        hna_ref[...] = _rmsnorm_bf16(xn_ref[...], g_ref[...])
        _attn_project(hnb_ref, w_ref, cos_ref[hi, :], sin_ref[hi, :], o_ref, hi)


def _attn_inproj(lay, h, g, w):
    cos, sin, rope_pair = lay.rope_tables(PAIR)
    tile_b, tile_n, _, pair_out = _pair_index_maps(lay.tiles - 1)
    x_tile = (TILE, D_MODEL)
    rope_spec = pl.BlockSpec((PAIR, HEAD_DIM), lambda s, rt: (rt[jnp.maximum(s - 1, 0)], 0))
    grid_spec = pltpu.PrefetchScalarGridSpec(
        num_scalar_prefetch=1,
        grid=(lay.pairs + 1,),
        in_specs=[pl.BlockSpec(x_tile, tile_b), pl.BlockSpec(x_tile, tile_n),
                  _resident((1, D_MODEL)), _resident((D_MODEL, ATTN_IN)), rope_spec, rope_spec],
        out_specs=pl.BlockSpec((PAIR, ATTN_IN), pair_out),
        scratch_shapes=[pltpu.VMEM(x_tile, BF16), pltpu.VMEM(x_tile, BF16)],
    )
    return pl.pallas_call(
        _attn_inproj_kernel,
        out_shape=jax.ShapeDtypeStruct((lay.rows, ATTN_IN), BF16),
        grid_spec=grid_spec,
        compiler_params=_params(1),
        name="attn_inproj",
    )(rope_pair, h, h, g, w, cos, sin)


_REDUCE_SLABS = 5


def _reduce_rows(x, op):
    slabs = x.reshape(_REDUCE_SLABS, x.shape[0] // _REDUCE_SLABS, x.shape[1])
    return op(op(slabs, axis=0), axis=0, keepdims=True)


_BLOCKS_PER_STEP = 4


def _attn_core_kernel(prev_ref, next_ref, lead_ref, case_ref, sink_ref, *refs):
    n_in = 2 + 4 * _BLOCKS_PER_STEP
    qz_ref, kvo_ref = refs[:2]
    bias_refs = refs[2:2 + _BLOCKS_PER_STEP]
    nbr_refs = refs[2 + _BLOCKS_PER_STEP:n_in]
    o_ref = refs[n_in]

    def heads_of(h):
        return [slice((h * GROUP + g) * HEAD_DIM, (h * GROUP + g + 1) * HEAD_DIM)
                for g in range(GROUP)]

    def keys_of(blk, col0, h):
        kvp_ref, kvn_ref, kvm_ref = nbr_refs[3 * blk:3 * blk + 3]
        cols = slice(col0 + h * HEAD_DIM, col0 + (h + 1) * HEAD_DIM)
        own = kvo_ref[blk * BLOCK:(blk + 1) * BLOCK, cols]
        return jnp.concatenate([kvp_ref[:, cols], own, kvn_ref[:, cols], kvm_ref[:, cols]], axis=0)

    scores = {}
    outs = {}
    for blk in range(_BLOCKS_PER_STEP):
        rows = slice(blk * BLOCK, (blk + 1) * BLOCK)
        for h in range(N_KV_HEADS):
            qs = jnp.concatenate([qz_ref[rows, cols] for cols in heads_of(h)], axis=0)
            scores[blk, h] = lax.dot_general(keys_of(blk, 0, h), qs, (((1,), (1,)), ((), ())),
                                             preferred_element_type=F32)
        bias = bias_refs[blk][0]
        for h in range(N_KV_HEADS):
            probs, inv = [], []
            for g in range(GROUP):
                s = scores[blk, h][:, g * BLOCK:(g + 1) * BLOCK] + bias
                sink = sink_ref[h * GROUP + g] * LOG2E
                m = jnp.maximum(_reduce_rows(s, jnp.max), sink)
                e = jnp.exp2(s - m)
                inv.append(1.0 / (_reduce_rows(e, jnp.sum) + jnp.exp2(sink - m)))
                probs.append(e.astype(BF16))
            o_all = lax.dot_general(keys_of(blk, KV_WIDTH, h), jnp.concatenate(probs, axis=1),
                                    (((0,), (0,)), ((), ())), preferred_element_type=F32)
            outs[blk, h] = o_all * jnp.concatenate(inv, axis=1)
    for blk in range(_BLOCKS_PER_STEP):
        rows = slice(blk * BLOCK, (blk + 1) * BLOCK)
        for h in range(N_KV_HEADS):
            for g, cols in enumerate(heads_of(h)):
                o = outs[blk, h][:, g * BLOCK:(g + 1) * BLOCK].T
                gate = qz_ref[rows, Z_COL0 + cols.start:Z_COL0 + cols.stop].astype(F32)
                o_ref[rows, cols] = (o * gate).astype(BF16)


def _attn_core(lay, proj, sink, tables, biases):
    n = _BLOCKS_PER_STEP
    assert lay.blocks % n == 0 and K_COL0 % (2 * KV_WIDTH) == 0 and V_COL0 == K_COL0 + KV_WIDTH
    kv_col = K_COL0 // (2 * KV_WIDTH)
    meta_blk = (LEAD - N_META) // BF16_SUBLANES
    per_lead = BLOCK // BF16_SUBLANES

    def step_rows(col):
        return lambda i, pv, nx, ld, cs: (i, col)

    def bias_of(k):
        return lambda i, pv, nx, ld, cs: (cs[n * i + k], 0, 0)

    def prev(k):
        return lambda i, pv, nx, ld, cs: (pv[n * i + k], kv_col)

    def nxt(k):
        return lambda i, pv, nx, ld, cs: (nx[n * i + k], kv_col)

    def meta(k):
        return lambda i, pv, nx, ld, cs: (ld[n * i + k] * per_lead + meta_blk, kv_col)

    kv_blk = (BLOCK, 2 * KV_WIDTH)
    nbr_specs = []
    for k in range(n):
        nbr_specs += [pl.BlockSpec(kv_blk, prev(k)), pl.BlockSpec(kv_blk, nxt(k)),
                      pl.BlockSpec((N_META, 2 * KV_WIDTH), meta(k))]
    grid_spec = pltpu.PrefetchScalarGridSpec(
        num_scalar_prefetch=4,
        grid=(lay.blocks // n,),
        in_specs=[pl.BlockSpec(memory_space=pltpu.SMEM),
                  pl.BlockSpec((n * BLOCK, K_COL0), step_rows(0)),
                  pl.BlockSpec((n * BLOCK, 2 * KV_WIDTH), step_rows(kv_col))]
        + [pl.BlockSpec((1,) + biases.shape[1:], bias_of(k)) for k in range(n)] + nbr_specs,
        out_specs=pl.BlockSpec((n * BLOCK, ATTN_WIDTH), step_rows(0)),
    )
    return pl.pallas_call(
        _attn_core_kernel,
        out_shape=jax.ShapeDtypeStruct((lay.rows, ATTN_WIDTH), BF16),
        grid_spec=grid_spec,
        compiler_params=_params(1),
        name="attn_core",
    )(*tables, sink, proj, proj, *([biases] * n), *([proj] * (3 * n)))


def _finish(h_new, fw_ref):
    if fw_ref is None:
        return h_new
    ms = jnp.mean(h_new * h_new, axis=-1, keepdims=True)
    return (h_new * lax.rsqrt(ms + EPS)) * fw_ref[...]


def _attn_outproj_kernel(a_ref, w_ref, h_ref, o_ref):
    o_ref[...] = h_ref[...] + jnp.dot(a_ref[...], w_ref[...], preferred_element_type=F32)


def _attn_outproj(lay, a, w, h):
    rows = pl.BlockSpec((PAIR, D_MODEL), lambda i: (i, 0))
    return pl.pallas_call(
        _attn_outproj_kernel,
        out_shape=jax.ShapeDtypeStruct((lay.rows, D_MODEL), F32),
        grid=(lay.pairs,),
        in_specs=[rows, _resident((ATTN_WIDTH, D_MODEL)), rows],
        out_specs=rows,
        compiler_params=_params(1),
        name="attn_outproj",
    )(a, w, h)


def _conv_project(hn_ref, w_ref, ok, cu_ref, t_ref, rows):
    for c in range(CONV_WIDTH // COL_CHUNK):
        def part(p):
            cols = slice(p * CONV_WIDTH + c * COL_CHUNK, p * CONV_WIDTH + (c + 1) * COL_CHUNK)
            return jnp.dot(hn_ref[...], w_ref[:, cols], preferred_element_type=F32)
        cols = slice(c * COL_CHUNK, (c + 1) * COL_CHUNK)
        cu_ref[rows, cols] = jnp.where(ok, part(1) * part(2), 0.0).astype(BF16)
        t_ref[rows, cols] = (part(0) * _silu(part(3))).astype(BF16)


def _conv_inproj_kernel(xb_ref, xn_ref, g_ref, w_ref, cu_ref, t_ref, hna_ref, hnb_ref, *,
                        main_pairs):
    s = pl.program_id(0)

    @pl.when(s == 0)
    def _():
        hna_ref[...] = _rmsnorm_bf16(xn_ref[...], g_ref[...])
        cu_ref[...] = jnp.zeros_like(cu_ref)
        t_ref[...] = jnp.zeros_like(t_ref)

    @pl.when(s > 0)
    def _():
        r = lax.broadcasted_iota(jnp.int32, (TILE, 1), 0)
        first_ok = jnp.where(s - 1 < main_pairs, 0, LEAD - N_META)
        ok = r % BLOCK >= first_ok
        hnb_ref[...] = _rmsnorm_bf16(xb_ref[...], g_ref[...])
        _conv_project(hna_ref, w_ref, ok, cu_ref, t_ref, slice(0, TILE))
        hna_ref[...] = _rmsnorm_bf16(xn_ref[...], g_ref[...])
        _conv_project(hnb_ref, w_ref, ok, cu_ref, t_ref, slice(TILE, PAIR))


def _conv_inproj(lay, h, g, w):
    tile_b, tile_n, _, pair_out = _pair_index_maps(lay.tiles - 1)
    x_tile = (TILE, D_MODEL)
    out = jax.ShapeDtypeStruct((lay.rows, CONV_WIDTH), BF16)
    out_spec = pl.BlockSpec((PAIR, CONV_WIDTH), pair_out)
    return pl.pallas_call(
        functools.partial(_conv_inproj_kernel, main_pairs=lay.main_rows // PAIR),
        out_shape=(out, out),
        grid=(lay.pairs + 1,),
        in_specs=[pl.BlockSpec(x_tile, tile_b), pl.BlockSpec(x_tile, tile_n),
                  _resident((1, D_MODEL)), _resident((D_MODEL, 4 * CONV_WIDTH))],
        out_specs=(out_spec, out_spec),
        scratch_shapes=[pltpu.VMEM(x_tile, BF16), pltpu.VMEM(x_tile, BF16)],
        compiler_params=_params(1),
        name="conv_inproj",
    )(h, h, g, w)


_N_CONV_IN = 4 + LEADS_PER_TILE
F32_SUBLANES = 8


def _conv_gate(tile, zero_next_ref, is_lead_ref, refs, cw_ref, g_ref):
    cu_ref, cup_ref, cun_ref = refs[:3]
    first_refs = refs[3:3 + LEADS_PER_TILE]
    t_ref = refs[3 + LEADS_PER_TILE]
    sub = F32_SUBLANES
    r8 = lax.broadcasted_iota(jnp.int32, (sub, CONV_WIDTH), 0)
    is_lead = is_lead_ref[tile] == 1
    prev_row = cup_ref[BF16_SUBLANES - 1:, :].astype(F32)
    next_row = jnp.where(zero_next_ref[tile] == 1, 0.0, cun_ref[:1, :].astype(F32))
    x = cu_ref[...].astype(F32)

    for k, f_ref in enumerate(first_refs):
        first_tok = jnp.where(is_lead, f_ref[:1, :].astype(F32), 0.0)
        at = (k + 1) * BLOCK
        if at < TILE:
            patch_row = jnp.where(is_lead, 0, -1)
            patched = jnp.where(r8 == patch_row, first_tok, x[at:at + sub])
            x = jnp.concatenate([x[:at], patched, x[at + sub:]], axis=0)
        else:
            next_row = jnp.where(is_lead, first_tok, next_row)

    up = pltpu.roll(x, 1, axis=0)
    dn = pltpu.roll(x, TILE - 1, axis=0)
    up = jnp.concatenate([jnp.where(r8 == 0, prev_row, up[:sub]), up[sub:]], axis=0)
    dn = jnp.concatenate([dn[:TILE - sub], jnp.where(r8 == sub - 1, next_row, dn[TILE - sub:])],
                         axis=0)
    y = up * cw_ref[0:1, :] + x * cw_ref[1:2, :] + dn * cw_ref[2:3, :]
    g_ref[...] = (t_ref[...].astype(F32) * y).astype(BF16)


def _conv_outproj_kernel(*refs, pair0, last_tile, final):
    n_tbl = 4 + LEADS_PER_TILE
    zero_next_ref, is_lead_ref = refs[2], refs[3]
    refs = refs[n_tbl:]
    b_refs, n_refs = refs[:_N_CONV_IN], refs[_N_CONV_IN:2 * _N_CONV_IN]
    cw_ref, w_ref, h_ref = refs[2 * _N_CONV_IN:2 * _N_CONV_IN + 3]
    rest = refs[2 * _N_CONV_IN + 3:]
    if final:
        fw_ref, o_ref, ga_ref, gb_ref = rest
    else:
        fw_ref = None
        o_ref, ga_ref, gb_ref = rest
    s = pl.program_id(0)
    tile_b = jnp.clip(2 * (s + pair0) - 1, 0, last_tile)
    tile_n = jnp.minimum(2 * (s + pair0), last_tile)

    def project(g_ref, rows):
        h_new = h_ref[rows, :] + jnp.dot(g_ref[...], w_ref[...], preferred_element_type=F32)
        o_ref[rows, :] = _finish(h_new, fw_ref)

    @pl.when(s == 0)
    def _():
        _conv_gate(tile_n, zero_next_ref, is_lead_ref, n_refs, cw_ref, ga_ref)
        o_ref[...] = jnp.zeros_like(o_ref)

    @pl.when(s > 0)
    def _():
        _conv_gate(tile_b, zero_next_ref, is_lead_ref, b_refs, cw_ref, gb_ref)
        project(ga_ref, slice(0, TILE))
        _conv_gate(tile_n, zero_next_ref, is_lead_ref, n_refs, cw_ref, ga_ref)
        project(gb_ref, slice(TILE, PAIR))


def _conv_outproj(lay, cu, t, cw, w, h, tables, pair0, n_pairs, fw=None):
    last_tile = lay.tiles - 1
    tile_b, tile_n, pair_in, pair_out = _pair_index_maps(last_tile, pair0)
    halo = (BF16_SUBLANES, CONV_WIDTH)
    tile_blk = (TILE, CONV_WIDTH)

    def table_spec(k, tile_of):
        return pl.BlockSpec(halo, lambda s, *tb: (tb[k][tile_of(s)[0]], 0))

    def tile_specs(tile_of):
        return ([pl.BlockSpec(tile_blk, tile_of), table_spec(0, tile_of), table_spec(1, tile_of)]
                + [table_spec(4 + k, tile_of) for k in range(LEADS_PER_TILE)]
                + [pl.BlockSpec(tile_blk, tile_of)])

    def resident(shape):
        return pl.BlockSpec(shape, lambda s, *tb: (0,) * len(shape), pipeline_mode=pl.Buffered(1))

    tile_args = [cu] * (3 + LEADS_PER_TILE) + [t]
    in_specs = (tile_specs(tile_b) + tile_specs(tile_n)
                + [resident((3, CONV_WIDTH)), resident((CONV_WIDTH, D_MODEL)),
                   pl.BlockSpec((PAIR, D_MODEL), pair_in)])
    args = tile_args + tile_args + [cw, w, h]
    if fw is not None:
        in_specs.append(resident((1, D_MODEL)))
        args.append(fw)
    grid_spec = pltpu.PrefetchScalarGridSpec(
        num_scalar_prefetch=len(tables),
        grid=(n_pairs + 1,),
        in_specs=in_specs,
        out_specs=pl.BlockSpec((PAIR, D_MODEL), pair_out),
        scratch_shapes=[pltpu.VMEM(tile_blk, BF16), pltpu.VMEM(tile_blk, BF16)],
    )
    return pl.pallas_call(
        functools.partial(_conv_outproj_kernel, pair0=pair0, last_tile=last_tile,
                          final=fw is not None),
        out_shape=jax.ShapeDtypeStruct((n_pairs * PAIR, D_MODEL), F32),
        grid_spec=grid_spec,
        compiler_params=_params(1),
        name="conv_outproj",
    )(*tables, *args)


def _attn_w_in_bf16(w):
    kv0, z0 = ATTN_WIDTH, ATTN_WIDTH + 2 * KV_WIDTH
    return jnp.concatenate([w[:, :kv0], w[:, z0:], w[:, kv0:z0]], axis=1).astype(BF16)


def kernel(x_prompt, x_sample, meta_tokens, norm_w, attn_w_in, attn_w_out, attn_sink,
           conv_w_in, conv_w, conv_w_out, final_norm_w):
    d = D_MODEL
    groups = (x_prompt, x_sample)
    lay = Layout([x.shape[:2] for x in groups])

    lead = jnp.concatenate([jnp.zeros((LEAD - N_META, d), F32), meta_tokens.astype(F32)], axis=0)
    srcs = [x.reshape(-1, d) for x in groups] + [jnp.tile(lead, (LEADS_PER_TILE, 1))]

    blk_tables, key_biases = lay.block_tables()
    conv_tables = lay.conv_tables()
    assert len(conv_tables) == 4 + LEADS_PER_TILE

    depth = norm_w.shape[0]
    assert depth % 2 == 0
    h = None
    outs = None
    for i in range(depth):
        g = norm_w[i][None, :]
        j = i // 2
        if i % 2 == 0:
            w_in = _attn_w_in_bf16(attn_w_in[j])
            if h is None:
                proj, h = _attn_inproj_first(lay, srcs, g, w_in)
            else:
                proj = _attn_inproj(lay, h, g, w_in)
            a = _attn_core(lay, proj, attn_sink[j], blk_tables, key_biases)
            h = _attn_outproj(lay, a, attn_w_out[j].astype(BF16), h)
        else:
            cu, t = _conv_inproj(lay, h, g, conv_w_in[j].astype(BF16))
            w_out = conv_w_out[j].astype(BF16)
            if i < depth - 1:
                h = _conv_outproj(lay, cu, t, conv_w[j], w_out, h, conv_tables, 0, lay.pairs)
            else:
                outs, pair0 = [], 0
                for x, rows in zip(groups, lay.group_rows):
                    n_pairs = rows // PAIR
                    y = _conv_outproj(lay, cu, t, conv_w[j], w_out, h, conv_tables, pair0,
                                      n_pairs, fw=final_norm_w[None, :])
                    outs.append(y.reshape(x.shape))
                    pair0 += n_pairs
    return tuple(outs)
```

```python
import functools

import jax
import jax.numpy as jnp
import numpy as np
from jax import lax
from jax.experimental import pallas as pl
from jax.experimental.pallas import tpu as pltpu

D_MODEL = 2048
HEAD_DIM = 128
N_Q_HEADS = 16
N_KV_HEADS = 4
GROUP = N_Q_HEADS // N_KV_HEADS
ATTN_WIDTH = N_Q_HEADS * HEAD_DIM
KV_WIDTH = N_KV_HEADS * HEAD_DIM
ATTN_IN = 2 * ATTN_WIDTH + 2 * KV_WIDTH
CONV_WIDTH = D_MODEL
WINDOW = 128
BLOCK = 128
N_META = 16
LEAD = BLOCK
ROPE_THETA = 10000.0
EPS = 1e-6
NEG = -1e30
LOG2E = 1.4426950408889634

TILE = 256
PAIR = 2 * TILE
COL_CHUNK = 512
BF16_SUBLANES = 16
LEADS_PER_TILE = TILE // BLOCK
VMEM_LIMIT = 58 * 1024 * 1024
Q_SCALE = HEAD_DIM ** -0.5 * LOG2E

Z_COL0 = ATTN_WIDTH
K_COL0 = 2 * ATTN_WIDTH
V_COL0 = 2 * ATTN_WIDTH + KV_WIDTH

F32 = jnp.float32
BF16 = jnp.bfloat16


def _params(n_axes):
    return pltpu.CompilerParams(
        dimension_semantics=("arbitrary",) * n_axes, vmem_limit_bytes=VMEM_LIMIT)


def _layer_resident(stack, layer):
    return pl.BlockSpec((None,) + stack.shape[1:], lambda *_: (layer, 0, 0),
                        pipeline_mode=pl.Buffered(1))


def _resident(shape):
    return pl.BlockSpec(shape, lambda *_: (0,) * len(shape), pipeline_mode=pl.Buffered(1))


def _rmsnorm_bf16(x, g):
    ms = jnp.mean(x * x, axis=-1, keepdims=True)
    return ((x * lax.rsqrt(ms + EPS)) * g).astype(BF16)


def _silu(z):
    return z * jax.nn.sigmoid(z)


def _key_bias(j, n):
    nk = 3 * BLOCK + N_META
    r = np.arange(BLOCK)[:, None]
    c = np.arange(nk)[None, :]
    part, cc = c // BLOCK, c % BLOCK
    pos_q = j * BLOCK + r - (LEAD - N_META)
    kblk = j - 1 + part
    pos_k = kblk * BLOCK + cc - (LEAD - N_META)
    band = (kblk >= 0) & (kblk < n) & (pos_k >= 0) & (np.abs(pos_q - pos_k) <= WINDOW)
    meta = np.abs(pos_q - cc) > WINDOW
    visible = np.where(part == 3, meta, band)
    return np.where(visible, 0.0, NEG).astype(np.float32).T


class Layout:
    def __init__(self, group_shapes):
        self.seq_rows = []
        self.group_rows = []
        for bsz, s in group_shapes:
            assert s % PAIR == 0
            self.seq_rows += [s] * bsz
            self.group_rows.append(bsz * s)
        self.n_seq = len(self.seq_rows)
        self.seq_row0 = np.concatenate([[0], np.cumsum(self.seq_rows)[:-1]]).astype(int)
        self.main_rows = int(sum(self.seq_rows))
        per_pair = PAIR // LEAD
        self.n_lead = -(-self.n_seq // per_pair) * per_pair
        self.rows = self.main_rows + self.n_lead * LEAD
        self.tiles = self.rows // TILE
        self.pairs = self.rows // PAIR
        self.main_blocks = self.main_rows // BLOCK
        self.blocks = self.rows // BLOCK

    def lead_row0(self, s):
        return self.main_rows + s * LEAD

    def block_tables(self):
        prev, nxt, lead, case = (np.zeros(self.blocks, np.int32) for _ in range(4))
        biases = []

        def bias_case(j, n):
            bias = _key_bias(j, n)
            for k, known in enumerate(biases):
                if np.array_equal(known, bias):
                    return k
            biases.append(bias)
            return len(biases) - 1

        for s in range(self.n_lead):
            lb = self.main_blocks + s
            real = s < self.n_seq
            first = self.seq_row0[s] // BLOCK if real else lb
            n = self.seq_rows[s] // BLOCK + 1 if real else 1
            prev[lb], nxt[lb], lead[lb], case[lb] = lb, first, lb, bias_case(0, n)
            for j in range(1, n):
                b = first + j - 1
                prev[b] = lb if j == 1 else b - 1
                nxt[b] = b + 1 if j < n - 1 else b
                lead[b], case[b] = lb, bias_case(j, n)
        tables = tuple(jnp.asarray(t) for t in (prev, nxt, lead, case))
        return tables, jnp.asarray(np.stack(biases))

    def conv_tables(self):
        per = TILE // BF16_SUBLANES
        main_tiles = self.main_rows // TILE
        prev = np.zeros(self.tiles, np.int32)
        nxt = np.zeros(self.tiles, np.int32)
        zero_next = np.zeros(self.tiles, np.int32)
        is_lead = np.zeros(self.tiles, np.int32)
        first = np.zeros((LEADS_PER_TILE, self.tiles), np.int32)
        starts = {int(r): s for s, r in enumerate(self.seq_row0)}
        ends = {int(r + n) for r, n in zip(self.seq_row0, self.seq_rows)}
        for i in range(self.tiles):
            row0 = i * TILE
            nxt[i] = min((i + 1) * per, self.rows // BF16_SUBLANES - 1)
            if i < main_tiles:
                if row0 in starts:
                    prev[i] = (self.lead_row0(starts[row0]) + LEAD) // BF16_SUBLANES - 1
                else:
                    prev[i] = i * per - 1
                zero_next[i] = int(row0 + TILE in ends)
            else:
                prev[i] = i * per - 1
                is_lead[i] = 1
                for k in range(LEADS_PER_TILE):
                    s = (i - main_tiles) * LEADS_PER_TILE + k
                    first[k, i] = self.seq_row0[s] // BF16_SUBLANES if s < self.n_seq else 0
        return tuple(jnp.asarray(t) for t in (prev, nxt, zero_next, is_lead, *first))

    def rope_tables(self, tm):
        longest = max(self.seq_rows)
        pos_main = np.arange(longest) + N_META
        pos_lead = np.tile(np.arange(LEAD) - (LEAD - N_META), PAIR // LEAD)
        pos = np.concatenate([pos_main, pos_lead]).astype(np.float64)
        half = HEAD_DIM // 2
        inv_freq = ROPE_THETA ** (-np.arange(0, half, dtype=np.float64) * (2.0 / HEAD_DIM))
        ang = pos[:, None] * inv_freq[None, :]
        cos = np.concatenate([np.cos(ang), np.cos(ang)], -1).astype(np.float32)
        sin = np.concatenate([-np.sin(ang), np.sin(ang)], -1).astype(np.float32)
        tile = np.zeros(self.rows // tm, np.int32)
        for s in range(self.n_seq):
            t0 = self.seq_row0[s] // tm
            for k in range(self.seq_rows[s] // tm):
                tile[t0 + k] = k
        lead_tiles = np.arange(self.rows // tm - self.main_rows // tm)
        tile[self.main_rows // tm:] = longest // tm + lead_tiles % (PAIR // tm)
        return jnp.asarray(cos), jnp.asarray(sin), jnp.asarray(tile)


def _attn_project(hn_ref, w_ref, cos, sin, o_ref, rows):
    cos_q = cos * Q_SCALE
    sin_q = sin * Q_SCALE
    for c in range(ATTN_IN // COL_CHUNK):
        cols = slice(c * COL_CHUNK, (c + 1) * COL_CHUNK)
        acc = jnp.dot(hn_ref[...], w_ref[:, cols], preferred_element_type=F32)
        col0 = c * COL_CHUNK
        is_q = col0 < Z_COL0
        if is_q or K_COL0 <= col0 < V_COL0:
            cs, sn = (cos_q, sin_q) if is_q else (cos, sin)
            heads = []
            for hh in range(COL_CHUNK // HEAD_DIM):
                xh = acc[:, hh * HEAD_DIM:(hh + 1) * HEAD_DIM]
                heads.append(xh * cs + pltpu.roll(xh, HEAD_DIM // 2, axis=1) * sn)
            acc = jnp.concatenate(heads, axis=1)
        elif Z_COL0 <= col0 < K_COL0:
            acc = _silu(acc)
        o_ref[rows, cols] = acc.astype(BF16)


def _attn_inproj_first_kernel(tile_ref, *refs, src_tiles):
    n_src = len(src_tiles)
    x_refs = refs[:n_src]
    g_ref, w_ref, cos_ref, sin_ref, o_ref, h_ref, hn_ref = refs[n_src:]
    i = pl.program_id(0)
    t0 = 0
    for x_ref, nt in zip(x_refs, src_tiles):
        @pl.when((i >= t0) & (i < t0 + nt))
        def _(x_ref=x_ref):
            x = x_ref[...]
            hn_ref[...] = _rmsnorm_bf16(x, g_ref[...])
            h_ref[...] = x
        t0 += nt
    _attn_project(hn_ref, w_ref, cos_ref[...], sin_ref[...], o_ref, slice(None))


def _attn_inproj_first(lay, srcs, g, w):
    tm = TILE
    cos, sin, rope_tile = lay.rope_tables(tm)
    assert srcs[-1].shape[0] == tm
    src_tiles = [x.shape[0] // tm for x in srcs[:-1]]
    src_tiles.append(lay.tiles - sum(src_tiles))
    row_spec = pl.BlockSpec((tm, D_MODEL), lambda i, rt: (i, 0))
    x_specs, t0 = [], 0
    for x, nt in zip(srcs, src_tiles):
        last = x.shape[0] // tm - 1
        x_specs.append(pl.BlockSpec(
            (tm, D_MODEL), lambda i, rt, t0=t0, last=last: (jnp.clip(i - t0, 0, last), 0)))
        t0 += nt
    rope_spec = pl.BlockSpec((tm, HEAD_DIM), lambda i, rt: (rt[i], 0))
    grid_spec = pltpu.PrefetchScalarGridSpec(
        num_scalar_prefetch=1,
        grid=(lay.tiles,),
        in_specs=x_specs + [_resident((1, D_MODEL)), _layer_resident(*w),
                            rope_spec, rope_spec],
        out_specs=(pl.BlockSpec((tm, ATTN_IN), lambda i, rt: (i, 0)), row_spec),
        scratch_shapes=[pltpu.VMEM((tm, D_MODEL), BF16)],
    )
    return pl.pallas_call(
        functools.partial(_attn_inproj_first_kernel, src_tiles=tuple(src_tiles)),
        out_shape=(jax.ShapeDtypeStruct((lay.rows, ATTN_IN), BF16),
                   jax.ShapeDtypeStruct((lay.rows, D_MODEL), F32)),
        grid_spec=grid_spec,
        compiler_params=_params(1),
        name="attn_inproj_first",
    )(rope_tile, *srcs, g, w[0], cos, sin)


def _pair_index_maps(last_tile, pair0=0):
    def tile_b(s, *_):
        return (jnp.clip(2 * (s + pair0) - 1, 0, last_tile), 0)

    def tile_n(s, *_):
        return (jnp.minimum(2 * (s + pair0), last_tile), 0)

    def pair_in(s, *_):
        return (jnp.maximum(s - 1, 0) + pair0, 0)

    def pair_out(s, *_):
        return (jnp.maximum(s - 1, 0), 0)

    return tile_b, tile_n, pair_in, pair_out


def _attn_inproj_kernel(tile_ref, xb_ref, xn_ref, g_ref, w_ref, cos_ref, sin_ref, o_ref,
                        hna_ref, hnb_ref):
    s = pl.program_id(0)

    @pl.when(s == 0)
    def _():
        hna_ref[...] = _rmsnorm_bf16(xn_ref[...], g_ref[...])
        o_ref[...] = jnp.zeros_like(o_ref)

    @pl.when(s > 0)
    def _():
        lo, hi = slice(0, TILE), slice(TILE, PAIR)
        hnb_ref[...] = _rmsnorm_bf16(xb_ref[...], g_ref[...])
        _attn_project(hna_ref, w_ref, cos_ref[lo, :], sin_ref[lo, :], o_ref, lo)
        hna_ref[...] = _rmsnorm_bf16(xn_ref[...], g_ref[...])
        _attn_project(hnb_ref, w_ref, cos_ref[hi, :], sin_ref[hi, :], o_ref, hi)


def _attn_inproj(lay, h, g, w):
    cos, sin, rope_pair = lay.rope_tables(PAIR)
    tile_b, tile_n, _, pair_out = _pair_index_maps(lay.tiles - 1)
    x_tile = (TILE, D_MODEL)
    rope_spec = pl.BlockSpec((PAIR, HEAD_DIM), lambda s, rt: (rt[jnp.maximum(s - 1, 0)], 0))
    grid_spec = pltpu.PrefetchScalarGridSpec(
        num_scalar_prefetch=1,
        grid=(lay.pairs + 1,),
        in_specs=[pl.BlockSpec(x_tile, tile_b), pl.BlockSpec(x_tile, tile_n),
                  _resident((1, D_MODEL)), _layer_resident(*w), rope_spec, rope_spec],
        out_specs=pl.BlockSpec((PAIR, ATTN_IN), pair_out),
        scratch_shapes=[pltpu.VMEM(x_tile, BF16), pltpu.VMEM(x_tile, BF16)],
    )
    return pl.pallas_call(
        _attn_inproj_kernel,
        out_shape=jax.ShapeDtypeStruct((lay.rows, ATTN_IN), BF16),
        grid_spec=grid_spec,
        compiler_params=_params(1),
        name="attn_inproj",
    )(rope_pair, h, h, g, w[0], cos, sin)


_REDUCE_SLABS = 5


def _reduce_rows(x, op):
    slabs = x.reshape(_REDUCE_SLABS, x.shape[0] // _REDUCE_SLABS, x.shape[1])
    return op(op(slabs, axis=0), axis=0, keepdims=True)


_BLOCKS_PER_STEP = 4


def _attn_core_kernel(prev_ref, next_ref, lead_ref, case_ref, sink_ref, *refs):
    n_in = 2 + 4 * _BLOCKS_PER_STEP
    qz_ref, kvo_ref = refs[:2]
    bias_refs = refs[2:2 + _BLOCKS_PER_STEP]
    nbr_refs = refs[2 + _BLOCKS_PER_STEP:n_in]
    o_ref = refs[n_in]

    def heads_of(h):
        return [slice((h * GROUP + g) * HEAD_DIM, (h * GROUP + g + 1) * HEAD_DIM)
                for g in range(GROUP)]

    def keys_of(blk, col0, h):
        kvp_ref, kvn_ref, kvm_ref = nbr_refs[3 * blk:3 * blk + 3]
        cols = slice(col0 + h * HEAD_DIM, col0 + (h + 1) * HEAD_DIM)
        own = kvo_ref[blk * BLOCK:(blk + 1) * BLOCK, cols]
        return jnp.concatenate([kvp_ref[:, cols], own, kvn_ref[:, cols], kvm_ref[:, cols]], axis=0)

    scores = {}
    outs = {}
    for blk in range(_BLOCKS_PER_STEP):
        rows = slice(blk * BLOCK, (blk + 1) * BLOCK)
        for h in range(N_KV_HEADS):
            qs = jnp.concatenate([qz_ref[rows, cols] for cols in heads_of(h)], axis=0)
            scores[blk, h] = lax.dot_general(keys_of(blk, 0, h), qs, (((1,), (1,)), ((), ())),
                                             preferred_element_type=F32)
        bias = bias_refs[blk][0]
        for h in range(N_KV_HEADS):
            probs, inv = [], []
            for g in range(GROUP):
                s = scores[blk, h][:, g * BLOCK:(g + 1) * BLOCK] + bias
                sink = sink_ref[h * GROUP + g] * LOG2E
                m = jnp.maximum(_reduce_rows(s, jnp.max), sink)
                e = jnp.exp2(s - m)
                inv.append(1.0 / (_reduce_rows(e, jnp.sum) + jnp.exp2(sink - m)))
                probs.append(e.astype(BF16))
            o_all = lax.dot_general(keys_of(blk, KV_WIDTH, h), jnp.concatenate(probs, axis=1),
                                    (((0,), (0,)), ((), ())), preferred_element_type=F32)
            outs[blk, h] = o_all * jnp.concatenate(inv, axis=1)
    for blk in range(_BLOCKS_PER_STEP):
        rows = slice(blk * BLOCK, (blk + 1) * BLOCK)
        for h in range(N_KV_HEADS):
            for g, cols in enumerate(heads_of(h)):
                o = outs[blk, h][:, g * BLOCK:(g + 1) * BLOCK].T
                gate = qz_ref[rows, Z_COL0 + cols.start:Z_COL0 + cols.stop].astype(F32)
                o_ref[rows, cols] = (o * gate).astype(BF16)


def _attn_core(lay, proj, sink, tables, biases):
    n = _BLOCKS_PER_STEP
    assert lay.blocks % n == 0 and K_COL0 % (2 * KV_WIDTH) == 0 and V_COL0 == K_COL0 + KV_WIDTH
    kv_col = K_COL0 // (2 * KV_WIDTH)
    meta_blk = (LEAD - N_META) // BF16_SUBLANES
    per_lead = BLOCK // BF16_SUBLANES

    def step_rows(col):
        return lambda i, pv, nx, ld, cs: (i, col)

    def bias_of(k):
        return lambda i, pv, nx, ld, cs: (cs[n * i + k], 0, 0)

    def prev(k):
        return lambda i, pv, nx, ld, cs: (pv[n * i + k], kv_col)

    def nxt(k):
        return lambda i, pv, nx, ld, cs: (nx[n * i + k], kv_col)

    def meta(k):
        return lambda i, pv, nx, ld, cs: (ld[n * i + k] * per_lead + meta_blk, kv_col)

    kv_blk = (BLOCK, 2 * KV_WIDTH)
    nbr_specs = []
    for k in range(n):
        nbr_specs += [pl.BlockSpec(kv_blk, prev(k)), pl.BlockSpec(kv_blk, nxt(k)),
                      pl.BlockSpec((N_META, 2 * KV_WIDTH), meta(k))]
    grid_spec = pltpu.PrefetchScalarGridSpec(
        num_scalar_prefetch=4,
        grid=(lay.blocks // n,),
        in_specs=[pl.BlockSpec(memory_space=pltpu.SMEM),
                  pl.BlockSpec((n * BLOCK, K_COL0), step_rows(0)),
                  pl.BlockSpec((n * BLOCK, 2 * KV_WIDTH), step_rows(kv_col))]
        + [pl.BlockSpec((1,) + biases.shape[1:], bias_of(k)) for k in range(n)] + nbr_specs,
        out_specs=pl.BlockSpec((n * BLOCK, ATTN_WIDTH), step_rows(0)),
    )
    return pl.pallas_call(
        _attn_core_kernel,
        out_shape=jax.ShapeDtypeStruct((lay.rows, ATTN_WIDTH), BF16),
        grid_spec=grid_spec,
        compiler_params=_params(1),
        name="attn_core",
    )(*tables, sink, proj, proj, *([biases] * n), *([proj] * (3 * n)))


def _finish(h_new, fw_ref):
    if fw_ref is None:
        return h_new
    ms = jnp.mean(h_new * h_new, axis=-1, keepdims=True)
    return (h_new * lax.rsqrt(ms + EPS)) * fw_ref[...]


def _attn_outproj_kernel(a_ref, w_ref, h_ref, o_ref):
    o_ref[...] = h_ref[...] + jnp.dot(a_ref[...], w_ref[...], preferred_element_type=F32)


def _attn_outproj(lay, a, w, h):
    rows = pl.BlockSpec((PAIR, D_MODEL), lambda i: (i, 0))
    return pl.pallas_call(
        _attn_outproj_kernel,
        out_shape=jax.ShapeDtypeStruct((lay.rows, D_MODEL), F32),
        grid=(lay.pairs,),
        in_specs=[rows, _layer_resident(*w), rows],
        out_specs=rows,
        compiler_params=_params(1),
        name="attn_outproj",
    )(a, w[0], h)


def _conv_project(hn_ref, w_ref, ok, cu_ref, t_ref, rows):
    for c in range(CONV_WIDTH // COL_CHUNK):
        def part(p):
            cols = slice(p * CONV_WIDTH + c * COL_CHUNK, p * CONV_WIDTH + (c + 1) * COL_CHUNK)
            return jnp.dot(hn_ref[...], w_ref[:, cols], preferred_element_type=F32)
        cols = slice(c * COL_CHUNK, (c + 1) * COL_CHUNK)
        cu_ref[rows, cols] = jnp.where(ok, part(1) * part(2), 0.0).astype(BF16)
        t_ref[rows, cols] = (part(0) * _silu(part(3))).astype(BF16)


def _conv_inproj_kernel(xb_ref, xn_ref, g_ref, w_ref, cu_ref, t_ref, hna_ref, hnb_ref, *,
                        main_pairs):
    s = pl.program_id(0)

    @pl.when(s == 0)
    def _():
        hna_ref[...] = _rmsnorm_bf16(xn_ref[...], g_ref[...])
        cu_ref[...] = jnp.zeros_like(cu_ref)
        t_ref[...] = jnp.zeros_like(t_ref)

    @pl.when(s > 0)
    def _():
        r = lax.broadcasted_iota(jnp.int32, (TILE, 1), 0)
        first_ok = jnp.where(s - 1 < main_pairs, 0, LEAD - N_META)
        ok = r % BLOCK >= first_ok
        hnb_ref[...] = _rmsnorm_bf16(xb_ref[...], g_ref[...])
        _conv_project(hna_ref, w_ref, ok, cu_ref, t_ref, slice(0, TILE))
        hna_ref[...] = _rmsnorm_bf16(xn_ref[...], g_ref[...])
        _conv_project(hnb_ref, w_ref, ok, cu_ref, t_ref, slice(TILE, PAIR))


def _conv_inproj(lay, h, g, w):
    tile_b, tile_n, _, pair_out = _pair_index_maps(lay.tiles - 1)
    x_tile = (TILE, D_MODEL)
    out = jax.ShapeDtypeStruct((lay.rows, CONV_WIDTH), BF16)
    out_spec = pl.BlockSpec((PAIR, CONV_WIDTH), pair_out)
    return pl.pallas_call(
        functools.partial(_conv_inproj_kernel, main_pairs=lay.main_rows // PAIR),
        out_shape=(out, out),
        grid=(lay.pairs + 1,),
        in_specs=[pl.BlockSpec(x_tile, tile_b), pl.BlockSpec(x_tile, tile_n),
                  _resident((1, D_MODEL)), _layer_resident(*w)],
        out_specs=(out_spec, out_spec),
        scratch_shapes=[pltpu.VMEM(x_tile, BF16), pltpu.VMEM(x_tile, BF16)],
        compiler_params=_params(1),
        name="conv_inproj",
    )(h, h, g, w[0])


_N_CONV_IN = 4 + LEADS_PER_TILE
F32_SUBLANES = 8


def _conv_gate(tile, zero_next_ref, is_lead_ref, refs, cw_ref, g_ref):
    cu_ref, cup_ref, cun_ref = refs[:3]
    first_refs = refs[3:3 + LEADS_PER_TILE]
    t_ref = refs[3 + LEADS_PER_TILE]
    sub = F32_SUBLANES
    r8 = lax.broadcasted_iota(jnp.int32, (sub, CONV_WIDTH), 0)
    is_lead = is_lead_ref[tile] == 1
    prev_row = cup_ref[BF16_SUBLANES - 1:, :].astype(F32)
    next_row = jnp.where(zero_next_ref[tile] == 1, 0.0, cun_ref[:1, :].astype(F32))
    x = cu_ref[...].astype(F32)

    for k, f_ref in enumerate(first_refs):
        first_tok = jnp.where(is_lead, f_ref[:1, :].astype(F32), 0.0)
        at = (k + 1) * BLOCK
        if at < TILE:
            patch_row = jnp.where(is_lead, 0, -1)
            patched = jnp.where(r8 == patch_row, first_tok, x[at:at + sub])
            x = jnp.concatenate([x[:at], patched, x[at + sub:]], axis=0)
        else:
            next_row = jnp.where(is_lead, first_tok, next_row)

    up = pltpu.roll(x, 1, axis=0)
    dn = pltpu.roll(x, TILE - 1, axis=0)
    up = jnp.concatenate([jnp.where(r8 == 0, prev_row, up[:sub]), up[sub:]], axis=0)
    dn = jnp.concatenate([dn[:TILE - sub], jnp.where(r8 == sub - 1, next_row, dn[TILE - sub:])],
                         axis=0)
    y = up * cw_ref[0:1, :] + x * cw_ref[1:2, :] + dn * cw_ref[2:3, :]
    g_ref[...] = (t_ref[...].astype(F32) * y).astype(BF16)


def _conv_outproj_kernel(*refs, pair0, last_tile, final):
    n_tbl = 4 + LEADS_PER_TILE
    zero_next_ref, is_lead_ref = refs[2], refs[3]
    refs = refs[n_tbl:]
    b_refs, n_refs = refs[:_N_CONV_IN], refs[_N_CONV_IN:2 * _N_CONV_IN]
    cw_ref, w_ref, h_ref = refs[2 * _N_CONV_IN:2 * _N_CONV_IN + 3]
    rest = refs[2 * _N_CONV_IN + 3:]
    if final:
        fw_ref, o_ref, ga_ref, gb_ref = rest
    else:
        fw_ref = None
        o_ref, ga_ref, gb_ref = rest
    s = pl.program_id(0)
    tile_b = jnp.clip(2 * (s + pair0) - 1, 0, last_tile)
    tile_n = jnp.minimum(2 * (s + pair0), last_tile)

    def project(g_ref, rows):
        h_new = h_ref[rows, :] + jnp.dot(g_ref[...], w_ref[...], preferred_element_type=F32)
        o_ref[rows, :] = _finish(h_new, fw_ref)

    @pl.when(s == 0)
    def _():
        _conv_gate(tile_n, zero_next_ref, is_lead_ref, n_refs, cw_ref, ga_ref)
        o_ref[...] = jnp.zeros_like(o_ref)

    @pl.when(s > 0)
    def _():
        _conv_gate(tile_b, zero_next_ref, is_lead_ref, b_refs, cw_ref, gb_ref)
        project(ga_ref, slice(0, TILE))
        _conv_gate(tile_n, zero_next_ref, is_lead_ref, n_refs, cw_ref, ga_ref)
        project(gb_ref, slice(TILE, PAIR))


def _conv_outproj(lay, cu, t, cw, w, h, tables, pair0, n_pairs, fw=None):
    last_tile = lay.tiles - 1
    tile_b, tile_n, pair_in, pair_out = _pair_index_maps(last_tile, pair0)
    halo = (BF16_SUBLANES, CONV_WIDTH)
    tile_blk = (TILE, CONV_WIDTH)

    def table_spec(k, tile_of):
        return pl.BlockSpec(halo, lambda s, *tb: (tb[k][tile_of(s)[0]], 0))

    def tile_specs(tile_of):
        return ([pl.BlockSpec(tile_blk, tile_of), table_spec(0, tile_of), table_spec(1, tile_of)]
                + [table_spec(4 + k, tile_of) for k in range(LEADS_PER_TILE)]
                + [pl.BlockSpec(tile_blk, tile_of)])

    def resident(shape):
        return pl.BlockSpec(shape, lambda s, *tb: (0,) * len(shape), pipeline_mode=pl.Buffered(1))

    tile_args = [cu] * (3 + LEADS_PER_TILE) + [t]
    in_specs = (tile_specs(tile_b) + tile_specs(tile_n)
                + [resident((3, CONV_WIDTH)), _layer_resident(*w),
                   pl.BlockSpec((PAIR, D_MODEL), pair_in)])
    args = tile_args + tile_args + [cw, w[0], h]
    if fw is not None:
        in_specs.append(resident((1, D_MODEL)))
        args.append(fw)
    grid_spec = pltpu.PrefetchScalarGridSpec(
        num_scalar_prefetch=len(tables),
        grid=(n_pairs + 1,),
        in_specs=in_specs,
        out_specs=pl.BlockSpec((PAIR, D_MODEL), pair_out),
        scratch_shapes=[pltpu.VMEM(tile_blk, BF16), pltpu.VMEM(tile_blk, BF16)],
    )
    return pl.pallas_call(
        functools.partial(_conv_outproj_kernel, pair0=pair0, last_tile=last_tile,
                          final=fw is not None),
        out_shape=jax.ShapeDtypeStruct((n_pairs * PAIR, D_MODEL), F32),
        grid_spec=grid_spec,
        compiler_params=_params(1),
        name="conv_outproj",
    )(*tables, *args)


def _attn_w_in_bf16(w):
    kv0, z0 = ATTN_WIDTH, ATTN_WIDTH + 2 * KV_WIDTH
    return jnp.concatenate([w[..., :kv0], w[..., z0:], w[..., kv0:z0]], axis=-1).astype(BF16)


def kernel(x_prompt, x_sample, meta_tokens, norm_w, attn_w_in, attn_w_out, attn_sink,
           conv_w_in, conv_w, conv_w_out, final_norm_w):
    d = D_MODEL
    groups = (x_prompt, x_sample)
    lay = Layout([x.shape[:2] for x in groups])

    lead = jnp.concatenate([jnp.zeros((LEAD - N_META, d), F32), meta_tokens.astype(F32)], axis=0)
    srcs = [x.reshape(-1, d) for x in groups] + [jnp.tile(lead, (LEADS_PER_TILE, 1))]

    blk_tables, key_biases = lay.block_tables()
    conv_tables = lay.conv_tables()
    assert len(conv_tables) == 4 + LEADS_PER_TILE

    depth = norm_w.shape[0]
    assert depth % 2 == 0
    attn_w_in_b = _attn_w_in_bf16(attn_w_in)
    attn_w_out_b = attn_w_out.astype(BF16)
    conv_w_in_b = conv_w_in.astype(BF16)
    conv_w_out_b = conv_w_out.astype(BF16)
    h = None
    outs = None
    for i in range(depth):
        g = norm_w[i][None, :]
        j = i // 2
        if i % 2 == 0:
            w_in = (attn_w_in_b, j)
            if h is None:
                proj, h = _attn_inproj_first(lay, srcs, g, w_in)
            else:
                proj = _attn_inproj(lay, h, g, w_in)
            a = _attn_core(lay, proj, attn_sink[j], blk_tables, key_biases)
            h = _attn_outproj(lay, a, (attn_w_out_b, j), h)
        else:
            cu, t = _conv_inproj(lay, h, g, (conv_w_in_b, j))
            w_out = (conv_w_out_b, j)
            if i < depth - 1:
                h = _conv_outproj(lay, cu, t, conv_w[j], w_out, h, conv_tables, 0, lay.pairs)
            else:
                outs, pair0 = [], 0
                for x, rows in zip(groups, lay.group_rows):
                    n_pairs = rows // PAIR
                    y = _conv_outproj(lay, cu, t, conv_w[j], w_out, h, conv_tables, pair0,
                                      n_pairs, fw=final_norm_w[None, :])
                    outs.append(y.reshape(x.shape))
                    pair0 += n_pairs
    return tuple(outs)
```
